```python
import math
import jax, jax.numpy as jnp
from jax import lax
import numpy as np

D_MODEL = 4096
BATCH = 8
SEQ = 2048
DEPTH = 4

HEAD_DIM = 64
N_Q_HEADS = (D_MODEL // 2) // HEAD_DIM
N_KV_HEADS = 8
GQA_GROUP = N_Q_HEADS // N_KV_HEADS
ATTN_WIDTH = N_Q_HEADS * HEAD_DIM
KV_WIDTH = N_KV_HEADS * HEAD_DIM
WINDOW = 128
BLOCK = WINDOW
SGU_WIDTH = D_MODEL - ATTN_WIDTH
SGU_GROUPS = 8
SGU_GROUP_DIM = SGU_WIDTH // SGU_GROUPS
CHUNK = 128
MIX_WIDTH = ATTN_WIDTH + SGU_WIDTH
IN_WIDTH = ATTN_WIDTH + 2 * KV_WIDTH + 2 * SGU_WIDTH
D_FF = 11008
CONV_WIDTH = 3
N_BUCKETS = 32
MAX_DISTANCE = 128
N_MOD = 6
EPS = 1e-6

kernel_name = "hybrid_swa_sink_sgu_convffn_adaln"


def _rms(x, w):
    xf = x.astype(jnp.float32)
    y = xf * lax.rsqrt(jnp.mean(xf * xf, axis=-1, keepdims=True) + EPS)
    return (y * w.astype(jnp.float32)).astype(x.dtype)


def _layernorm(x, w, b):
    xf = x.astype(jnp.float32)
    mu = jnp.mean(xf, axis=-1, keepdims=True)
    xc = xf - mu
    var = jnp.mean(xc * xc, axis=-1, keepdims=True)
    y = xc * lax.rsqrt(var + EPS) * w.astype(jnp.float32) + b.astype(jnp.float32)
    return y.astype(x.dtype)


def _t5_causal_bucket(dist):
    max_exact = N_BUCKETS // 2
    n = jnp.maximum(dist, 0)
    nf = jnp.maximum(n, 1).astype(jnp.float32)
    large = max_exact + (jnp.log(nf / max_exact) / math.log(MAX_DISTANCE / max_exact)
                         * (N_BUCKETS - max_exact)).astype(jnp.int32)
    large = jnp.minimum(large, N_BUCKETS - 1)
    return jnp.where(n < max_exact, n, large)


def _swa_sink_attention(q, k, v, sinks, rel_bias):
    B, S = q.shape[0], q.shape[1]
    nb = S // BLOCK
    qb = q.reshape(B, nb, BLOCK, N_KV_HEADS, GQA_GROUP, HEAD_DIM)
    kb = k.reshape(B, nb, BLOCK, N_KV_HEADS, HEAD_DIM)
    vb = v.reshape(B, nb, BLOCK, N_KV_HEADS, HEAD_DIM)
    pad = ((0, 0), (1, 0), (0, 0), (0, 0), (0, 0))
    kcat = jnp.concatenate([jnp.pad(kb, pad)[:, :-1], kb], axis=2)
    vcat = jnp.concatenate([jnp.pad(vb, pad)[:, :-1], vb], axis=2)
    s = jnp.einsum('bnqhgd,bnkhd->bnhgqk', qb, kcat).astype(jnp.float32) * (HEAD_DIM ** -0.5)
    qi = jnp.arange(BLOCK)[:, None]
    kj = jnp.arange(2 * BLOCK)[None, :]
    dist = qi + BLOCK - kj
    in_window = (dist >= 0) & (dist < WINDOW)
    bias = rel_bias.astype(jnp.float32)[_t5_causal_bucket(dist)]
    bias = bias.transpose(2, 0, 1).reshape(N_KV_HEADS, GQA_GROUP, BLOCK, 2 * BLOCK)
    key_valid = (jnp.arange(nb)[:, None] > 0) | (kj >= BLOCK)
    mask = in_window[None] & key_valid[:, None, :]
    s = jnp.where(mask[None, :, None, None], s + bias, -1e30)
    sink = jnp.broadcast_to(sinks.astype(jnp.float32).reshape(1, 1, N_KV_HEADS, GQA_GROUP, 1, 1),
                            s.shape[:-1] + (1,))
    p = jax.nn.softmax(jnp.concatenate([s, sink], axis=-1), axis=-1)[..., :-1]
    o = jnp.einsum('bnhgqk,bnkhd->bnqhgd', p.astype(v.dtype), vcat)
    return o.reshape(B, S, ATTN_WIDTH)


def _chunked_sgu(u, v, ln_w, ln_b, w_s, b_s):
    B, S = u.shape[0], u.shape[1]
    nc = S // CHUNK
    vn = _layernorm(v, ln_w, ln_b).reshape(B, nc, CHUNK, SGU_GROUPS, SGU_GROUP_DIM)
    causal = jnp.tril(jnp.ones((CHUNK, CHUNK), dtype=bool))
    ws = jnp.where(causal[None], w_s, 0.0).astype(v.dtype)
    mixed = jnp.einsum('gts,bcsgd->bctgd', ws, vn) + b_s.T.astype(v.dtype)[None, None, :, :, None]
    return u * mixed.reshape(B, S, SGU_WIDTH)


def _causal_dwconv(h, w, b):
    S = h.shape[1]
    hp = jnp.pad(h, ((0, 0), (CONV_WIDTH - 1, 0), (0, 0)))
    return hp[:, 0:S] * w[0] + hp[:, 1:S + 1] * w[1] + hp[:, 2:S + 2] * w[2] + b


def setup_inputs(seed: int = 0) -> dict:
    key = jax.random.key(seed)
    ks = jax.random.split(key, 20)
    f32 = jnp.float32
    n = lambda k, shape, s: (jax.random.normal(k, shape, f32) * s)
    return {
        "x": n(ks[0], (BATCH, SEQ, D_MODEL), 1.0),
        "c": n(ks[1], (BATCH, D_MODEL), 1.0),
        "w_ada": n(ks[2], (D_MODEL, N_MOD * D_MODEL), D_MODEL ** -0.5),
        "b_ada": n(ks[3], (N_MOD * D_MODEL,), 0.01),
        "ada_table": n(ks[4], (DEPTH, N_MOD, D_MODEL), 0.02),
        "rel_bias": n(ks[5], (N_BUCKETS, N_Q_HEADS), 0.5),
        "norm_mix_w": 1.0 + n(ks[6], (DEPTH, D_MODEL), 0.02),
        "w_in": n(ks[7], (DEPTH, D_MODEL, IN_WIDTH), D_MODEL ** -0.5),
        "q_norm_w": 1.0 + n(ks[8], (DEPTH, HEAD_DIM), 0.02),
        "k_norm_w": 1.0 + n(ks[9], (DEPTH, HEAD_DIM), 0.02),
        "sinks": n(ks[10], (DEPTH, N_Q_HEADS), 0.5),
        "sgu_ln_w": 1.0 + n(ks[11], (DEPTH, SGU_WIDTH), 0.02),
        "sgu_ln_b": n(ks[12], (DEPTH, SGU_WIDTH), 0.01),
        "w_spatial": n(ks[13], (DEPTH, SGU_GROUPS, CHUNK, CHUNK), CHUNK ** -0.5),
        "b_spatial": 1.0 + n(ks[14], (DEPTH, SGU_GROUPS, CHUNK), 0.1),
        "w_out": n(ks[15], (DEPTH, MIX_WIDTH, D_MODEL), MIX_WIDTH ** -0.5),
        "norm_ffn_w": 1.0 + n(ks[16], (DEPTH, D_MODEL), 0.02),
        "w_up": n(ks[17], (DEPTH, D_MODEL, 2 * D_FF), D_MODEL ** -0.5),
        "conv_w": n(ks[18], (DEPTH, CONV_WIDTH, 2 * D_FF), CONV_WIDTH ** -0.5),
        "conv_b": n(jax.random.fold_in(ks[18], 1), (DEPTH, 2 * D_FF), 0.01),
        "w_down": n(ks[19], (DEPTH, D_FF, D_MODEL), D_FF ** -0.5),
    }


def reference(x, c, w_ada, b_ada, ada_table, rel_bias, norm_mix_w, w_in, q_norm_w, k_norm_w,
              sinks, sgu_ln_w, sgu_ln_b, w_spatial, b_spatial, w_out, norm_ffn_w, w_up,
              conv_w, conv_b, w_down):
    B, S, D = x.shape
    mod = (jax.nn.silu(c) @ w_ada + b_ada).reshape(B, N_MOD, D)
    splits = [ATTN_WIDTH, ATTN_WIDTH + KV_WIDTH, ATTN_WIDTH + 2 * KV_WIDTH,
              ATTN_WIDTH + 2 * KV_WIDTH + SGU_WIDTH]
    for l in range(DEPTH):
        m = mod + ada_table[l][None]
        sh_a, sc_a, g_a = m[:, 0, None, :], m[:, 1, None, :], m[:, 2, None, :]
        sh_f, sc_f, g_f = m[:, 3, None, :], m[:, 4, None, :], m[:, 5, None, :]

        h = _rms(x, norm_mix_w[l]) * (1.0 + sc_a) + sh_a
        z = h @ w_in[l]
        q, k, v_att, u, v_sgu = jnp.split(z, splits, axis=-1)
        q = _rms(q.reshape(B, S, N_Q_HEADS, HEAD_DIM), q_norm_w[l])
        k = _rms(k.reshape(B, S, N_KV_HEADS, HEAD_DIM), k_norm_w[l])
        v_att = v_att.reshape(B, S, N_KV_HEADS, HEAD_DIM)
        y_att = _swa_sink_attention(q, k, v_att, sinks[l], rel_bias)
        y_sgu = _chunked_sgu(jax.nn.gelu(u, approximate=False),
                             jax.nn.gelu(v_sgu, approximate=False),
                             sgu_ln_w[l], sgu_ln_b[l], w_spatial[l], b_spatial[l])
        y = jnp.concatenate([y_att, y_sgu], axis=-1) @ w_out[l]
        x = x + g_a * y

        h = _rms(x, norm_ffn_w[l]) * (1.0 + sc_f) + sh_f
        a = _causal_dwconv(h @ w_up[l], conv_w[l], conv_b[l])
        a_gate, a_val = jnp.split(a, 2, axis=-1)
        x = x + g_f * ((jax.nn.silu(a_gate) * a_val) @ w_down[l])
    return x
```

```python
import functools
import math

import jax
import jax.numpy as jnp
from jax import lax
from jax.experimental import pallas as pl
from jax.experimental.pallas import tpu as pltpu

D_MODEL = 4096
BATCH = 8
SEQ = 2048
DEPTH = 4
HEAD_DIM = 64
N_Q_HEADS = (D_MODEL // 2) // HEAD_DIM
N_KV_HEADS = 8
GQA_GROUP = N_Q_HEADS // N_KV_HEADS
ATTN_WIDTH = N_Q_HEADS * HEAD_DIM
KV_WIDTH = N_KV_HEADS * HEAD_DIM
WINDOW = 128
BLOCK = WINDOW
SGU_WIDTH = D_MODEL - ATTN_WIDTH
SGU_GROUPS = 8
SGU_GROUP_DIM = SGU_WIDTH // SGU_GROUPS
CHUNK = 128
MIX_WIDTH = ATTN_WIDTH + SGU_WIDTH
IN_WIDTH = ATTN_WIDTH + 2 * KV_WIDTH + 2 * SGU_WIDTH
D_FF = 11008
CONV_WIDTH = 3
N_BUCKETS = 32
MAX_DISTANCE = 128
N_MOD = 6
EPS = 1e-6
NEG_INF = -1e30

TOKENS = BATCH * SEQ
F32 = jnp.float32
BF16 = jnp.bfloat16

V7X_VMEM_BYTES = 64 * 2**20
BF16_SUBLANES = 16
MIB = 2**20

ROW_TILE = 512
NORM_CHUNK = 16
MIX_IN_COLS = 1024
MIX_OUT_ROWS = 1024
MIX_OUT_COLS = 512
FF_BLOCK = 256
HALO = BF16_SUBLANES
ADA_COLS = 512


def _vmem_limit(block_bytes):
    return int(min(V7X_VMEM_BYTES - 6 * MIB, block_bytes))


def _adaln_rows(x, norm_w, shift, scale):
    ms = jnp.mean(x * x, axis=-1, keepdims=True)
    y = x * lax.rsqrt(ms + EPS) * norm_w
    return y * (1.0 + scale) + shift


def _ada_kernel(c_ref, w_ref, b_ref, tab_ref, o_ref):
    c = c_ref[...]
    a = c * jax.nn.sigmoid(c)
    mod = jnp.dot(a, w_ref[...], preferred_element_type=F32, precision=lax.Precision.HIGHEST) + b_ref[...]
    for l in range(DEPTH):
        o_ref[l] = mod + tab_ref[l:l + 1, :]


def _ada_mod(c, w_ada, b_ada, ada_table):
    width = N_MOD * D_MODEL
    return pl.pallas_call(
        _ada_kernel,
        grid=(width // ADA_COLS,),
        in_specs=[
            pl.BlockSpec((BATCH, D_MODEL), lambda j: (0, 0)),
            pl.BlockSpec((D_MODEL, ADA_COLS), lambda j: (0, j)),
            pl.BlockSpec((1, ADA_COLS), lambda j: (0, j)),
            pl.BlockSpec((DEPTH, ADA_COLS), lambda j: (0, j)),
        ],
        out_specs=pl.BlockSpec((DEPTH, BATCH, ADA_COLS), lambda j: (0, 0, j)),
        out_shape=jax.ShapeDtypeStruct((DEPTH, BATCH, width), F32),
        compiler_params=pltpu.CompilerParams(
            dimension_semantics=("arbitrary",),
            vmem_limit_bytes=_vmem_limit(4 * D_MODEL * ADA_COLS * 4 + 8 * MIB)),
        name="ada_mod",
    )(c, w_ada, b_ada.reshape(1, width), ada_table.reshape(DEPTH, width))


def _t5_causal_bucket(dist):
    max_exact = N_BUCKETS // 2
    n = jnp.maximum(dist, 0)
    nf = jnp.maximum(n, 1).astype(jnp.float32)
    large = max_exact + (jnp.log(nf / max_exact) / math.log(MAX_DISTANCE / max_exact)
                         * (N_BUCKETS - max_exact)).astype(jnp.int32)
    large = jnp.minimum(large, N_BUCKETS - 1)
    return jnp.where(n < max_exact, n, large)


def _bias_kernel(rb_ref, bucket_ref, o_ref):
    h = pl.program_id(0)
    bucket = bucket_ref[...]
    acc = jnp.full(bucket.shape, NEG_INF, F32)
    for b in range(N_BUCKETS):
        acc = jnp.where(bucket == b, rb_ref[b, h], acc)
    o_ref[0] = acc


def _band_bias(rel_bias):
    qi = jnp.arange(BLOCK)[:, None]
    kj = jnp.arange(2 * BLOCK)[None, :]
    dist = qi + BLOCK - kj
    in_window = (dist >= 0) & (dist < WINDOW)
    bucket = jnp.where(in_window, _t5_causal_bucket(dist), -1).astype(jnp.int32)
    return pl.pallas_call(
        _bias_kernel,
        grid=(N_Q_HEADS,),
        in_specs=[
            pl.BlockSpec(memory_space=pltpu.SMEM),
            pl.BlockSpec((BLOCK, 2 * BLOCK), lambda h: (0, 0)),
        ],
        out_specs=pl.BlockSpec((1, BLOCK, 2 * BLOCK), lambda h: (h, 0, 0)),
        out_shape=jax.ShapeDtypeStruct((N_Q_HEADS, BLOCK, 2 * BLOCK), F32),
        name="band_bias",
    )(rel_bias, bucket)


def _norm_into(x_ref, h_ref, h_row0, rows, norm_w, shift, scale):
    def body(r, carry):
        src = pl.multiple_of(r * NORM_CHUNK, NORM_CHUNK)
        h = _adaln_rows(x_ref[pl.ds(src, NORM_CHUNK), :], norm_w, shift, scale)
        h_ref[pl.ds(pl.multiple_of(h_row0 + src, NORM_CHUNK), NORM_CHUNK), :] = h.astype(BF16)
        return carry

    lax.fori_loop(0, rows // NORM_CHUNK, body, 0)


def _mix_in_kernel(x_ref, mod_ref, nw_ref, w_ref, o_ref, h_ref):
    @pl.when(pl.program_id(1) == 0)
    def _():
        _norm_into(x_ref, h_ref, 0, ROW_TILE, nw_ref[...], mod_ref[0:1, :], mod_ref[1:2, :])

    o_ref[...] = jnp.dot(h_ref[...], w_ref[...], preferred_element_type=F32).astype(o_ref.dtype)


def _mix_in(x, mod, norm_w, w_in):
    tiles_per_seq = SEQ // ROW_TILE
    blocks = (2 * ROW_TILE * D_MODEL * 4 + ROW_TILE * D_MODEL * 2
              + 2 * D_MODEL * MIX_IN_COLS * 2 + 4 * ROW_TILE * MIX_IN_COLS * 4)
    return pl.pallas_call(
        _mix_in_kernel,
        grid=(TOKENS // ROW_TILE, IN_WIDTH // MIX_IN_COLS),
        in_specs=[
            pl.BlockSpec((ROW_TILE, D_MODEL), lambda i, j: (i, 0)),
            pl.BlockSpec((None, N_MOD, D_MODEL), lambda i, j: (i // tiles_per_seq, 0, 0)),
            pl.BlockSpec((1, D_MODEL), lambda i, j: (0, 0)),
            pl.BlockSpec((D_MODEL, MIX_IN_COLS), lambda i, j: (0, j)),
        ],
        out_specs=pl.BlockSpec((ROW_TILE, MIX_IN_COLS), lambda i, j: (i, j)),
        out_shape=jax.ShapeDtypeStruct((TOKENS, IN_WIDTH), BF16),
        scratch_shapes=[pltpu.VMEM((ROW_TILE, D_MODEL), BF16)],
        compiler_params=pltpu.CompilerParams(
            dimension_semantics=("arbitrary", "arbitrary"),
            vmem_limit_bytes=_vmem_limit(blocks + 4 * MIB)),
        name="mix_in",
    )(x, mod, norm_w.reshape(1, D_MODEL), w_in)


Z_Q_BLOCK = 0
Z_U_BLOCK = 1
Z_VS_BLOCK = 2
Z_K_BLOCK = (ATTN_WIDTH + 2 * SGU_WIDTH) // KV_WIDTH
Z_V_BLOCK = Z_K_BLOCK + 1


def _head_rms(t, w):
    return t * lax.rsqrt(jnp.mean(t * t, axis=-1, keepdims=True) + EPS) * w


def _attention(q_ref, kc_ref, kp_ref, vc_ref, vp_ref, bias_ref, sinks_ref, qnw_ref, knw_ref, o_ref):
    first_block = pl.program_id(1) == 0
    qnw = qnw_ref[...] * (HEAD_DIM ** -0.5)
    knw = knw_ref[...]
    for hk in range(N_KV_HEADS):
        cols = slice(hk * HEAD_DIM, (hk + 1) * HEAD_DIM)
        k = jnp.concatenate([kp_ref[:, cols], kc_ref[:, cols]], axis=0).astype(F32)
        k = _head_rms(k, knw).astype(BF16)
        v = jnp.concatenate([vp_ref[:, cols], vc_ref[:, cols]], axis=0)
        for g in range(GQA_GROUP):
            hq = hk * GQA_GROUP + g
            qcols = slice(hq * HEAD_DIM, (hq + 1) * HEAD_DIM)
            q = _head_rms(q_ref[:, qcols].astype(F32), qnw).astype(BF16)
            s = lax.dot_general(q, k, (((1,), (1,)), ((), ())), preferred_element_type=F32)
            bias = bias_ref[hq]
            prev_key = lax.broadcasted_iota(jnp.int32, bias.shape, 1) < BLOCK
            s = s + jnp.where(first_block & prev_key, NEG_INF, bias)
            sink = sinks_ref[hq]
            m = jnp.maximum(jnp.max(s, axis=-1, keepdims=True), sink)
            p = jnp.exp(s - m)
            denom = jnp.sum(p, axis=-1, keepdims=True) + jnp.exp(sink - m)
            o = jnp.dot(p.astype(BF16), v, preferred_element_type=F32)
            o_ref[:, qcols] = (o / denom).astype(o_ref.dtype)


def _gelu(t):
    return 0.5 * t * (1.0 + lax.erf(t * (2.0 ** -0.5)))


def _sgu(u_ref, vs_ref, lnw_ref, lnb_ref, ws_ref, bst_ref, o_ref):
    v = _gelu(vs_ref[...].astype(F32))
    mu = jnp.mean(v, axis=-1, keepdims=True)
    vc = v - mu
    var = jnp.mean(vc * vc, axis=-1, keepdims=True)
    vn = (vc * lax.rsqrt(var + EPS) * lnw_ref[...] + lnb_ref[...]).astype(BF16)
    row = lax.broadcasted_iota(jnp.int32, (CHUNK, CHUNK), 0)
    col = lax.broadcasted_iota(jnp.int32, (CHUNK, CHUNK), 1)
    causal = col <= row
    for g in range(SGU_GROUPS):
        cols = slice(g * SGU_GROUP_DIM, (g + 1) * SGU_GROUP_DIM)
        ws = jnp.where(causal, ws_ref[g], 0.0).astype(BF16)
        mixed = jnp.dot(ws, vn[:, cols], preferred_element_type=F32) + bst_ref[:, g:g + 1]
        u = _gelu(u_ref[:, cols].astype(F32))
        o_ref[:, ATTN_WIDTH + g * SGU_GROUP_DIM:ATTN_WIDTH + (g + 1) * SGU_GROUP_DIM] = (
            u * mixed).astype(o_ref.dtype)


def _mixer_kernel(q_ref, kc_ref, kp_ref, vc_ref, vp_ref, u_ref, vs_ref, bias_ref, sinks_ref, qnw_ref, knw_ref,
                  lnw_ref, lnb_ref, ws_ref, bst_ref, o_ref):
    _attention(q_ref, kc_ref, kp_ref, vc_ref, vp_ref, bias_ref, sinks_ref, qnw_ref, knw_ref, o_ref)
    _sgu(u_ref, vs_ref, lnw_ref, lnb_ref, ws_ref, bst_ref, o_ref)


def _mixer(z, band_bias, sinks, q_norm_w, k_norm_w, sgu_ln_w, sgu_ln_b, w_spatial, b_spatial):
    nb = SEQ // BLOCK
    cur = lambda b, n: b * nb + n
    prev = lambda b, n: b * nb + jnp.maximum(n - 1, 0)
    whole = lambda *shape: pl.BlockSpec(shape, lambda b, n: (0,) * len(shape))
    return pl.pallas_call(
        _mixer_kernel,
        grid=(BATCH, nb),
        in_specs=[
            pl.BlockSpec((BLOCK, ATTN_WIDTH), lambda b, n: (cur(b, n), Z_Q_BLOCK)),
            pl.BlockSpec((BLOCK, KV_WIDTH), lambda b, n: (cur(b, n), Z_K_BLOCK)),
            pl.BlockSpec((BLOCK, KV_WIDTH), lambda b, n: (prev(b, n), Z_K_BLOCK)),
            pl.BlockSpec((BLOCK, KV_WIDTH), lambda b, n: (cur(b, n), Z_V_BLOCK)),
            pl.BlockSpec((BLOCK, KV_WIDTH), lambda b, n: (prev(b, n), Z_V_BLOCK)),
            pl.BlockSpec((CHUNK, SGU_WIDTH), lambda b, n: (cur(b, n), Z_U_BLOCK)),
            pl.BlockSpec((CHUNK, SGU_WIDTH), lambda b, n: (cur(b, n), Z_VS_BLOCK)),
            whole(N_Q_HEADS, BLOCK, 2 * BLOCK),
            pl.BlockSpec(memory_space=pltpu.SMEM),
            whole(1, HEAD_DIM),
            whole(1, HEAD_DIM),
            whole(1, SGU_WIDTH),
            whole(1, SGU_WIDTH),
            whole(SGU_GROUPS, CHUNK, CHUNK),
            whole(CHUNK, SGU_GROUPS),
        ],
        out_specs=pl.BlockSpec((BLOCK, MIX_WIDTH), lambda b, n: (cur(b, n), 0)),
        out_shape=jax.ShapeDtypeStruct((TOKENS, MIX_WIDTH), BF16),
        compiler_params=pltpu.CompilerParams(
            dimension_semantics=("arbitrary", "arbitrary"),
            vmem_limit_bytes=_vmem_limit(40 * MIB)),
        name="mixer",
    )(z, z, z, z, z, z, z, band_bias, sinks, q_norm_w.reshape(1, HEAD_DIM), k_norm_w.reshape(1, HEAD_DIM),
      sgu_ln_w.reshape(1, SGU_WIDTH), sgu_ln_b.reshape(1, SGU_WIDTH), w_spatial, b_spatial.T)


def _mix_out_kernel(y_ref, w_ref, x_ref, mod_ref, o_ref):
    acc = jnp.dot(y_ref[...], w_ref[...], preferred_element_type=F32)
    o_ref[...] = x_ref[...] + mod_ref[2:3, :] * acc


def _mix_out(y, w_out, x, mod):
    tiles_per_seq = SEQ // MIX_OUT_ROWS
    blocks = (2 * MIX_OUT_ROWS * MIX_WIDTH * 2 + 2 * MIX_WIDTH * MIX_OUT_COLS * 2
              + 6 * MIX_OUT_ROWS * MIX_OUT_COLS * 4)
    return pl.pallas_call(
        _mix_out_kernel,
        grid=(TOKENS // MIX_OUT_ROWS, D_MODEL // MIX_OUT_COLS),
        in_specs=[
            pl.BlockSpec((MIX_OUT_ROWS, MIX_WIDTH), lambda i, j: (i, 0)),
            pl.BlockSpec((MIX_WIDTH, MIX_OUT_COLS), lambda i, j: (0, j)),
            pl.BlockSpec((MIX_OUT_ROWS, MIX_OUT_COLS), lambda i, j: (i, j)),
            pl.BlockSpec((None, N_MOD, MIX_OUT_COLS), lambda i, j: (i // tiles_per_seq, 0, j)),
        ],
        out_specs=pl.BlockSpec((MIX_OUT_ROWS, MIX_OUT_COLS), lambda i, j: (i, j)),
        out_shape=jax.ShapeDtypeStruct((TOKENS, D_MODEL), F32),
        compiler_params=pltpu.CompilerParams(
            dimension_semantics=("arbitrary", "arbitrary"),
            vmem_limit_bytes=_vmem_limit(blocks + 4 * MIB)),
        name="mix_out",
    )(y, w_out, x, mod)


def _causal_conv(a, cw_ref, cb_ref):
    return (a[HALO - 2:HALO - 2 + ROW_TILE] * cw_ref[0:1, :] + a[HALO - 1:HALO - 1 + ROW_TILE] * cw_ref[1:2, :]
            + a[HALO:HALO + ROW_TILE] * cw_ref[2:3, :] + cb_ref[...])


def _ffn_kernel(x_ref, xh_ref, mod_ref, nw_ref, wg_ref, wv_ref, cwg_ref, cwv_ref, cbg_ref, cbv_ref, wd_ref,
                o_ref, h_ref):
    i = pl.program_id(0)
    j = pl.program_id(1)

    @pl.when(j == 0)
    def _():
        nw, shift, scale = nw_ref[...], mod_ref[3:4, :], mod_ref[4:5, :]
        keep = jnp.where(i % (SEQ // ROW_TILE) == 0, 0.0, 1.0)
        h_ref[0:HALO, :] = (_adaln_rows(xh_ref[...], nw, shift, scale) * keep).astype(BF16)
        _norm_into(x_ref, h_ref, HALO, ROW_TILE, nw, shift, scale)
        o_ref[...] = jnp.zeros_like(o_ref)

    h = h_ref[...]
    a_gate = _causal_conv(jnp.dot(h, wg_ref[...], preferred_element_type=F32), cwg_ref, cbg_ref)
    a_val = _causal_conv(jnp.dot(h, wv_ref[...], preferred_element_type=F32), cwv_ref, cbv_ref)
    gated = (a_gate * jax.nn.sigmoid(a_gate) * a_val).astype(BF16)
    o_ref[...] += jnp.dot(gated, wd_ref[...], preferred_element_type=F32)

    @pl.when(j == pl.num_programs(1) - 1)
    def _():
        o_ref[...] = x_ref[...] + mod_ref[5:6, :] * o_ref[...]


def _ffn(x, mod, norm_w, w_up, conv_w, conv_b, w_down):
    tiles_per_seq = SEQ // ROW_TILE
    n_ff = D_FF // FF_BLOCK
    halo_blocks_per_tile = ROW_TILE // HALO
    conv_b = conv_b.reshape(1, 2 * D_FF)
    blocks = (4 * ROW_TILE * D_MODEL * 4 + (ROW_TILE + HALO) * D_MODEL * 2
              + 6 * D_MODEL * FF_BLOCK * 2 + 8 * (ROW_TILE + HALO) * FF_BLOCK * 4)
    return pl.pallas_call(
        _ffn_kernel,
        grid=(TOKENS // ROW_TILE, n_ff),
        in_specs=[
            pl.BlockSpec((ROW_TILE, D_MODEL), lambda i, j: (i, 0)),
            pl.BlockSpec((HALO, D_MODEL), lambda i, j: (jnp.maximum(i * halo_blocks_per_tile - 1, 0), 0)),
            pl.BlockSpec((None, N_MOD, D_MODEL), lambda i, j: (i // tiles_per_seq, 0, 0)),
            pl.BlockSpec((1, D_MODEL), lambda i, j: (0, 0)),
            pl.BlockSpec((D_MODEL, FF_BLOCK), lambda i, j: (0, j)),
            pl.BlockSpec((D_MODEL, FF_BLOCK), lambda i, j: (0, n_ff + j)),
            pl.BlockSpec((CONV_WIDTH, FF_BLOCK), lambda i, j: (0, j)),
            pl.BlockSpec((CONV_WIDTH, FF_BLOCK), lambda i, j: (0, n_ff + j)),
            pl.BlockSpec((1, FF_BLOCK), lambda i, j: (0, j)),
            pl.BlockSpec((1, FF_BLOCK), lambda i, j: (0, n_ff + j)),
            pl.BlockSpec((FF_BLOCK, D_MODEL), lambda i, j: (j, 0)),
        ],
        out_specs=pl.BlockSpec((ROW_TILE, D_MODEL), lambda i, j: (i, 0)),
        out_shape=jax.ShapeDtypeStruct((TOKENS, D_MODEL), F32),
        scratch_shapes=[pltpu.VMEM((HALO + ROW_TILE, D_MODEL), BF16)],
        compiler_params=pltpu.CompilerParams(
            dimension_semantics=("arbitrary", "arbitrary"),
            vmem_limit_bytes=_vmem_limit(blocks)),
        name="ffn",
    )(x, x, mod, norm_w.reshape(1, D_MODEL), w_up, w_up, conv_w, conv_w, conv_b, conv_b, w_down)


def _permute_w_in(w_in_l):
    kv0 = ATTN_WIDTH
    u0 = ATTN_WIDTH + 2 * KV_WIDTH
    return jnp.concatenate([w_in_l[:, :kv0], w_in_l[:, u0:], w_in_l[:, kv0:u0]], axis=1).astype(BF16)


def kernel(x, c, w_ada, b_ada, ada_table, rel_bias, norm_mix_w, w_in, q_norm_w, k_norm_w, sinks, sgu_ln_w,
           sgu_ln_b, w_spatial, b_spatial, w_out, norm_ffn_w, w_up, conv_w, conv_b, w_down):
    B, S, D = x.shape
    assert (B, S, D) == (BATCH, SEQ, D_MODEL)
    mods = _ada_mod(c, w_ada, b_ada, ada_table).reshape(DEPTH, BATCH, N_MOD, D_MODEL)
    band_bias = _band_bias(rel_bias)
    xt = x.reshape(TOKENS, D_MODEL)
    for l in range(DEPTH):
        mod = mods[l]
        z = _mix_in(xt, mod, norm_mix_w[l], _permute_w_in(w_in[l]))
        y = _mixer(z, band_bias, sinks[l], q_norm_w[l], k_norm_w[l], sgu_ln_w[l], sgu_ln_b[l], w_spatial[l],
                   b_spatial[l])
        xt = _mix_out(y, w_out[l].astype(BF16), xt, mod)
        xt = _ffn(xt, mod, norm_ffn_w[l], w_up[l].astype(BF16), conv_w[l], conv_b[l], w_down[l].astype(BF16))
    return xt.reshape(B, S, D)
```

```python
import jax
import jax.numpy as jnp
import math
from jax import lax
from jax.experimental import pallas as pl
from jax.experimental.pallas import tpu as pltpu

D_MODEL = 4096
BATCH = 8
SEQ = 2048
DEPTH = 4
HEAD_DIM = 64
N_Q_HEADS = (D_MODEL // 2) // HEAD_DIM
N_KV_HEADS = 8
GQA_GROUP = N_Q_HEADS // N_KV_HEADS
ATTN_WIDTH = N_Q_HEADS * HEAD_DIM
KV_WIDTH = N_KV_HEADS * HEAD_DIM
WINDOW = 128
BLOCK = WINDOW
SGU_WIDTH = D_MODEL - ATTN_WIDTH
SGU_GROUPS = 8
SGU_GROUP_DIM = SGU_WIDTH // SGU_GROUPS
CHUNK = 128
MIX_WIDTH = ATTN_WIDTH + SGU_WIDTH
IN_WIDTH = ATTN_WIDTH + 2 * KV_WIDTH + 2 * SGU_WIDTH
D_FF = 11008
CONV_WIDTH = 3
N_BUCKETS = 32
MAX_DISTANCE = 128
N_MOD = 6
EPS = 1e-6
NEG_INF = -1e30

TOKENS = BATCH * SEQ
F32 = jnp.float32
BF16 = jnp.bfloat16

V7X_VMEM_BYTES = 64 * 2**20
V7X_LANES = 128
V7X_MXU_DIM = 256
BF16_SUBLANES = 16
MIB = 2**20

ROW_TILE = 512
NORM_CHUNK = 16
NORM_UNROLL = 4
MIX_IN_COLS = 1024
MIX_OUT_ROWS = 1024
MIX_OUT_COLS = 512
FF_BLOCK = 256
N_FF = D_FF // FF_BLOCK
HALO = BF16_SUBLANES
GATE_ROWS = 32
FFN_PIECES = 8
ADA_COLS = 512
SGU_HALF = SGU_WIDTH // 2
PAIR = 2 * HEAD_DIM
QGROUP = GQA_GROUP * HEAD_DIM

assert PAIR == V7X_LANES and QGROUP == V7X_MXU_DIM and GQA_GROUP == 4
assert D_FF % FF_BLOCK == 0 and SEQ % ROW_TILE == 0 and SEQ % MIX_OUT_ROWS == 0


def _params(semantics, vmem_bytes):
    return pltpu.CompilerParams(dimension_semantics=semantics,
                                vmem_limit_bytes=int(min(V7X_VMEM_BYTES - 6 * MIB, vmem_bytes)))


def _ada_kernel(c_ref, w_ref, b_ref, tab_ref, o_ref):
    c = c_ref[...]
    a = c * jax.nn.sigmoid(c)
    mod = jnp.dot(a, w_ref[...], preferred_element_type=F32, precision=lax.Precision.HIGHEST) + b_ref[...]
    for l in range(DEPTH):
        o_ref[l] = mod + tab_ref[l:l + 1, :]


def _ada_mod(c, w_ada, b_ada, ada_table):
    width = N_MOD * D_MODEL
    return pl.pallas_call(
        _ada_kernel,
        grid=(width // ADA_COLS,),
        in_specs=[
            pl.BlockSpec((BATCH, D_MODEL), lambda j: (0, 0)),
            pl.BlockSpec((D_MODEL, ADA_COLS), lambda j: (0, j)),
            pl.BlockSpec((1, ADA_COLS), lambda j: (0, j)),
            pl.BlockSpec((DEPTH, ADA_COLS), lambda j: (0, j)),
        ],
        out_specs=pl.BlockSpec((DEPTH, BATCH, ADA_COLS), lambda j: (0, 0, j)),
        out_shape=jax.ShapeDtypeStruct((DEPTH, BATCH, width), F32),
        compiler_params=_params(("arbitrary",), 4 * D_MODEL * ADA_COLS * 4 + 8 * MIB),
        name="ada_mod",
    )(c, w_ada, b_ada.reshape(1, width), ada_table.reshape(DEPTH, width))


def _t5_causal_bucket(dist):
    max_exact = N_BUCKETS // 2
    n = jnp.maximum(dist, 0)
    nf = jnp.maximum(n, 1).astype(jnp.float32)
    large = max_exact + (jnp.log(nf / max_exact) / math.log(MAX_DISTANCE / max_exact)
                         * (N_BUCKETS - max_exact)).astype(jnp.int32)
    large = jnp.minimum(large, N_BUCKETS - 1)
    return jnp.where(n < max_exact, n, large)


def _bias_kernel(rb_ref, bucket_ref, o_ref):
    hk = pl.program_id(1)
    bucket = bucket_ref[...]
    for r in range(GQA_GROUP // 2):
        for par in range(2):
            hq = hk * GQA_GROUP + 2 * r + par
            acc = jnp.full(bucket.shape, NEG_INF, F32)
            for b in range(N_BUCKETS):
                acc = jnp.where(bucket == b, rb_ref[b, hq], acc)
            o_ref[r * BLOCK:(r + 1) * BLOCK, par * 2 * BLOCK:(par + 1) * 2 * BLOCK] = acc


def _band_bias(rel_bias):
    qi = jnp.arange(BLOCK)[:, None]
    kj = jnp.arange(2 * BLOCK)[None, :]
    dist = qi + BLOCK - kj
    in_window = (dist >= 0) & (dist < WINDOW)
    bucket = jnp.where(in_window, _t5_causal_bucket(dist), -1).astype(jnp.int32)
    buckets = jnp.stack([jnp.where(kj >= BLOCK, bucket, -1), bucket])
    return pl.pallas_call(
        _bias_kernel,
        grid=(2, N_KV_HEADS),
        in_specs=[
            pl.BlockSpec(memory_space=pltpu.SMEM),
            pl.BlockSpec((None, BLOCK, 2 * BLOCK), lambda v, h: (v, 0, 0)),
        ],
        out_specs=pl.BlockSpec((None, None, 2 * BLOCK, 4 * BLOCK), lambda v, h: (v, h, 0, 0)),
        out_shape=jax.ShapeDtypeStruct((2, N_KV_HEADS, 2 * BLOCK, 4 * BLOCK), F32),
        name="band_bias",
    )(rel_bias, buckets)


def _adaln_rows(x, gain, shift):
    ms = jnp.mean(x * x, axis=-1, keepdims=True)
    return x * lax.rsqrt(ms + EPS) * gain + shift


def _norm_into(x_ref, h_ref, h_row0, rows, gain, shift):
    def body(r, carry):
        src = pl.multiple_of(r * NORM_CHUNK, NORM_CHUNK)
        h = _adaln_rows(x_ref[pl.ds(src, NORM_CHUNK), :], gain, shift)
        h_ref[pl.ds(pl.multiple_of(h_row0 + src, NORM_CHUNK), NORM_CHUNK), :] = h.astype(BF16)
        return carry

    lax.fori_loop(0, rows // NORM_CHUNK, body, 0, unroll=NORM_UNROLL)


def _mix_in_kernel(x_ref, mod_ref, nw_ref, w_ref, o_ref, h_ref):
    @pl.when(pl.program_id(1) == 0)
    def _():
        _norm_into(x_ref, h_ref, 0, ROW_TILE, nw_ref[...] * (1.0 + mod_ref[1:2, :]), mod_ref[0:1, :])

    o_ref[...] = jnp.dot(h_ref[...], w_ref[...], preferred_element_type=F32).astype(o_ref.dtype)


def _mix_in(x, mod, norm_w, w_in):
    tiles_per_seq = SEQ // ROW_TILE
    blocks = (2 * ROW_TILE * D_MODEL * 4 + ROW_TILE * D_MODEL * 2
              + 2 * D_MODEL * MIX_IN_COLS * 2 + 4 * ROW_TILE * MIX_IN_COLS * 4)
    return pl.pallas_call(
        _mix_in_kernel,
        grid=(TOKENS // ROW_TILE, IN_WIDTH // MIX_IN_COLS),
        in_specs=[
            pl.BlockSpec((ROW_TILE, D_MODEL), lambda i, j: (i, 0)),
            pl.BlockSpec((None, N_MOD, D_MODEL), lambda i, j: (i // tiles_per_seq, 0, 0)),
            pl.BlockSpec((1, D_MODEL), lambda i, j: (0, 0)),
            pl.BlockSpec((D_MODEL, MIX_IN_COLS), lambda i, j: (0, j)),
        ],
        out_specs=pl.BlockSpec((ROW_TILE, MIX_IN_COLS), lambda i, j: (i, j)),
        out_shape=jax.ShapeDtypeStruct((TOKENS, IN_WIDTH), BF16),
        scratch_shapes=[pltpu.VMEM((ROW_TILE, D_MODEL), BF16)],
        compiler_params=_params(("arbitrary", "arbitrary"), blocks + 4 * MIB),
        name="mix_in",
    )(x, mod, norm_w.reshape(1, D_MODEL), w_in)


Z_K_BLOCK = ATTN_WIDTH // KV_WIDTH
Z_V_BLOCK = Z_K_BLOCK + 1
Z_U_BLOCK = (ATTN_WIDTH + 2 * KV_WIDTH) // SGU_HALF
Z_VS_BLOCK = Z_U_BLOCK + 2


def _grouped_rms(t, gain, head_ones):
    rows, width = t.shape
    n = width // QGROUP
    sq = t * t
    stacked = jnp.concatenate([sq[:, c * QGROUP:(c + 1) * QGROUP] for c in range(n)], axis=0).astype(BF16)
    ss = jnp.dot(stacked, head_ones, preferred_element_type=F32)
    rs = lax.rsqrt(ss * (1.0 / HEAD_DIM) + EPS)
    return [t[:, c * QGROUP:(c + 1) * QGROUP] * rs[c * rows:(c + 1) * rows] * gain for c in range(n)]


def _attention(q_ref, kc_ref, kp_ref, vc_ref, vp_ref, bias_ref, sinks_ref, qg_ref, kg_ref, ones_ref, o_ref):
    rows2 = 2 * BLOCK
    head_ones = ones_ref[...]
    lo = lax.broadcasted_iota(jnp.int32, (rows2, PAIR), 1) < HEAD_DIM
    top = lax.broadcasted_iota(jnp.int32, (rows2, 1), 0) < BLOCK
    ones_cols = jnp.concatenate([jnp.where(lo, 1.0, 0.0), jnp.where(lo, 0.0, 1.0)], axis=0).astype(BF16)

    qn = _grouped_rms(q_ref[...].astype(F32), qg_ref[...], head_ones)
    kn = _grouped_rms(jnp.concatenate([kp_ref[...], kc_ref[...]], axis=0).astype(F32), kg_ref[...], head_ones)
    vf = jnp.concatenate([vp_ref[...], vc_ref[...]], axis=0).astype(F32)

    def split_pair(pair):
        swapped = pltpu.roll(pair, HEAD_DIM, axis=1)
        a = jnp.concatenate([jnp.where(lo, pair, 0.0), jnp.where(lo, 0.0, swapped)], axis=0)
        b = jnp.concatenate([jnp.where(lo, swapped, 0.0), jnp.where(lo, 0.0, pair)], axis=0)
        return a.astype(BF16), b.astype(BF16)

    kk, vv = [], []
    for p in range(N_KV_HEADS // 2):
        kk.extend(split_pair(kn[p // 2][:, (p % 2) * PAIR:(p % 2 + 1) * PAIR]))
        vv.extend(split_pair(vf[:, p * PAIR:(p + 1) * PAIR]))

    def scores(hk):
        q2 = jnp.concatenate([qn[hk][:, :PAIR], qn[hk][:, PAIR:]], axis=0).astype(BF16)
        s = lax.dot_general(q2, kk[hk], (((1,), (1,)), ((), ())), preferred_element_type=F32)
        return s + bias_ref[hk]

    def softmax_pv(hk, s):
        probs, sink_terms = [], []
        for par in range(2):
            sp = s[:, par * rows2:(par + 1) * rows2]
            sink = jnp.where(top, sinks_ref[hk * GQA_GROUP + par], sinks_ref[hk * GQA_GROUP + 2 + par])
            m = jnp.maximum(jnp.max(sp, axis=-1, keepdims=True), sink)
            probs.append(jnp.exp(sp - m).astype(BF16))
            sink_terms.append(jnp.exp(sink - m))
        nd = jnp.dot(jnp.concatenate(probs, axis=1), jnp.concatenate([vv[hk], ones_cols], axis=1),
                     preferred_element_type=F32)
        den = nd[:, PAIR:] + jnp.where(lo, sink_terms[0], sink_terms[1])
        out = (nd[:, :PAIR] / den).astype(o_ref.dtype)
        o_ref[:, (2 * hk) * PAIR:(2 * hk + 1) * PAIR] = out[:BLOCK]
        o_ref[:, (2 * hk + 1) * PAIR:(2 * hk + 2) * PAIR] = out[BLOCK:]

    s_next = scores(0)
    for hk in range(N_KV_HEADS):
        s = s_next
        if hk + 1 < N_KV_HEADS:
            s_next = scores(hk + 1)
        softmax_pv(hk, s)


def _gelu(t):
    return 0.5 * t * (1.0 + lax.erf(t * (2.0 ** -0.5)))


def _sgu(u_refs, vs_refs, lnw_ref, lnb_ref, ws_ref, bst_ref, o_ref):
    v = _gelu(jnp.concatenate([r[...] for r in vs_refs], axis=1).astype(F32))
    mu = jnp.mean(v, axis=-1, keepdims=True)
    vc = v - mu
    var = jnp.mean(vc * vc, axis=-1, keepdims=True)
    vn = (vc * lax.rsqrt(var + EPS) * lnw_ref[...] + lnb_ref[...]).astype(BF16)
    row = lax.broadcasted_iota(jnp.int32, (CHUNK, CHUNK), 0)
    col = lax.broadcasted_iota(jnp.int32, (CHUNK, CHUNK), 1)
    causal = col <= row
    groups_per_ref = SGU_HALF // SGU_GROUP_DIM
    for g in range(SGU_GROUPS):
        ws = jnp.where(causal, ws_ref[g], 0.0).astype(BF16)
        mixed = (jnp.dot(ws, vn[:, g * SGU_GROUP_DIM:(g + 1) * SGU_GROUP_DIM], preferred_element_type=F32)
                 + bst_ref[:, g:g + 1])
        gr = g % groups_per_ref
        u = _gelu(u_refs[g // groups_per_ref][:, gr * SGU_GROUP_DIM:(gr + 1) * SGU_GROUP_DIM].astype(F32))
        o_ref[:, ATTN_WIDTH + g * SGU_GROUP_DIM:ATTN_WIDTH + (g + 1) * SGU_GROUP_DIM] = (
            u * mixed).astype(o_ref.dtype)


def _mixer_kernel(q_ref, kc_ref, kp_ref, vc_ref, vp_ref, u0_ref, u1_ref, vs0_ref, vs1_ref, bias_ref, sinks_ref,
                  qg_ref, kg_ref, ones_ref, lnw_ref, lnb_ref, ws_ref, bst_ref, o_ref):
    _attention(q_ref, kc_ref, kp_ref, vc_ref, vp_ref, bias_ref, sinks_ref, qg_ref, kg_ref, ones_ref, o_ref)
    _sgu((u0_ref, u1_ref), (vs0_ref, vs1_ref), lnw_ref, lnb_ref, ws_ref, bst_ref, o_ref)


def _mixer(z, band_bias, sinks, q_norm_w, k_norm_w, sgu_ln_w, sgu_ln_b, w_spatial, b_spatial):
    nb = SEQ // BLOCK
    cur = lambda b, n: b * nb + n
    prev = lambda b, n: b * nb + jnp.maximum(n - 1, 0)
    whole = lambda *shape: pl.BlockSpec(shape, lambda b, n: (0,) * len(shape))
    q_gain = jnp.tile(q_norm_w * (HEAD_DIM ** -0.5), GQA_GROUP).reshape(1, QGROUP)
    k_gain = jnp.tile(k_norm_w, GQA_GROUP).reshape(1, QGROUP)
    head_ones = jnp.kron(jnp.eye(GQA_GROUP, dtype=F32), jnp.ones((HEAD_DIM, HEAD_DIM), F32)).astype(BF16)
    return pl.pallas_call(
        _mixer_kernel,
        grid=(BATCH, nb),
        in_specs=[
            pl.BlockSpec((BLOCK, ATTN_WIDTH), lambda b, n: (cur(b, n), 0)),
            pl.BlockSpec((BLOCK, KV_WIDTH), lambda b, n: (cur(b, n), Z_K_BLOCK)),
            pl.BlockSpec((BLOCK, KV_WIDTH), lambda b, n: (prev(b, n), Z_K_BLOCK)),
            pl.BlockSpec((BLOCK, KV_WIDTH), lambda b, n: (cur(b, n), Z_V_BLOCK)),
            pl.BlockSpec((BLOCK, KV_WIDTH), lambda b, n: (prev(b, n), Z_V_BLOCK)),
            pl.BlockSpec((CHUNK, SGU_HALF), lambda b, n: (cur(b, n), Z_U_BLOCK)),
            pl.BlockSpec((CHUNK, SGU_HALF), lambda b, n: (cur(b, n), Z_U_BLOCK + 1)),
            pl.BlockSpec((CHUNK, SGU_HALF), lambda b, n: (cur(b, n), Z_VS_BLOCK)),
            pl.BlockSpec((CHUNK, SGU_HALF), lambda b, n: (cur(b, n), Z_VS_BLOCK + 1)),
            pl.BlockSpec((None, N_KV_HEADS, 2 * BLOCK, 4 * BLOCK), lambda b, n: (jnp.minimum(n, 1), 0, 0, 0)),
            pl.BlockSpec(memory_space=pltpu.SMEM),
            whole(1, QGROUP),
            whole(1, QGROUP),
            whole(QGROUP, QGROUP),
            whole(1, SGU_WIDTH),
            whole(1, SGU_WIDTH),
            whole(SGU_GROUPS, CHUNK, CHUNK),
            whole(CHUNK, SGU_GROUPS),
        ],
        out_specs=pl.BlockSpec((BLOCK, MIX_WIDTH), lambda b, n: (cur(b, n), 0)),
        out_shape=jax.ShapeDtypeStruct((TOKENS, MIX_WIDTH), BF16),
        compiler_params=_params(("arbitrary", "arbitrary"), 40 * MIB),
        name="mixer",
    )(z, z, z, z, z, z, z, z, z, band_bias, sinks, q_gain, k_gain, head_ones,
      sgu_ln_w.reshape(1, SGU_WIDTH), sgu_ln_b.reshape(1, SGU_WIDTH), w_spatial, b_spatial.T)


def _mix_out_kernel(y_ref, w_ref, x_ref, mod_ref, o_ref):
    acc = jnp.dot(y_ref[...], w_ref[...], preferred_element_type=F32)
    o_ref[...] = x_ref[...] + mod_ref[2:3, :] * acc


def _mix_out(y, w_out, x, mod):
    tiles_per_seq = SEQ // MIX_OUT_ROWS
    blocks = (2 * MIX_OUT_ROWS * MIX_WIDTH * 2 + 2 * MIX_WIDTH * MIX_OUT_COLS * 2
              + 6 * MIX_OUT_ROWS * MIX_OUT_COLS * 4)
    return pl.pallas_call(
        _mix_out_kernel,
        grid=(TOKENS // MIX_OUT_ROWS, D_MODEL // MIX_OUT_COLS),
        in_specs=[
            pl.BlockSpec((MIX_OUT_ROWS, MIX_WIDTH), lambda i, j: (i, 0)),
            pl.BlockSpec((MIX_WIDTH, MIX_OUT_COLS), lambda i, j: (0, j)),
            pl.BlockSpec((MIX_OUT_ROWS, MIX_OUT_COLS), lambda i, j: (i, j)),
            pl.BlockSpec((None, N_MOD, MIX_OUT_COLS), lambda i, j: (i // tiles_per_seq, 0, j)),
        ],
        out_specs=pl.BlockSpec((MIX_OUT_ROWS, MIX_OUT_COLS), lambda i, j: (i, j)),
        out_shape=jax.ShapeDtypeStruct((TOKENS, D_MODEL), F32),
        compiler_params=_params(("arbitrary", "arbitrary"), blocks + 4 * MIB),
        name="mix_out",
    )(y, w_out, x, mod)


def _ffn_kernel(x_ref, xh_ref, mod_ref, nw_ref, wu_ref, cw_ref, cb_ref, wd_ref, o_ref, h_ref, a0_ref, a1_ref, g0_ref,
                g1_ref):
    i = pl.program_id(0)
    j = pl.program_id(1)
    a_refs, g_refs = (a0_ref, a1_ref), (g0_ref, g1_ref)

    def up(slot):
        a_refs[slot][...] = jnp.dot(h_ref[...], wu_ref[...], preferred_element_type=F32)

    def gate(slot, piece):
        a_ref, g_ref = a_refs[slot], g_refs[slot]
        rows = ROW_TILE // FFN_PIECES
        for r0 in range(piece * rows, (piece + 1) * rows, GATE_ROWS):
            lag = lambda k: a_ref[HALO - k + r0:HALO - k + r0 + GATE_ROWS, :]
            a = lag(2) * cw_ref[0:1, :] + lag(1) * cw_ref[1:2, :] + lag(0) * cw_ref[2:3, :] + cb_ref[...]
            a_gate, a_val = a[:, :FF_BLOCK], a[:, FF_BLOCK:]
            g_ref[r0:r0 + GATE_ROWS, :] = (a_gate * jax.nn.sigmoid(a_gate) * a_val).astype(BF16)

    def down(slot, piece):
        cols = slice(piece * (D_MODEL // FFN_PIECES), (piece + 1) * (D_MODEL // FFN_PIECES))
        return cols, jnp.dot(g_refs[slot][...], wd_ref[:, cols], preferred_element_type=F32)

    def gate_and_down(gate_slot, down_slot):
        for piece in range(FFN_PIECES):
            cols, acc = down(down_slot, piece)
            o_ref[:, cols] += acc
            gate(gate_slot, piece)

    last_slot = (N_FF - 1) % 2

    @pl.when(j == 0)
    def _():
        gain, shift = nw_ref[...] * (1.0 + mod_ref[4:5, :]), mod_ref[3:4, :]
        keep = jnp.where(i % (SEQ // ROW_TILE) == 0, 0.0, 1.0)
        h_ref[0:HALO, :] = (_adaln_rows(xh_ref[...], gain, shift) * keep).astype(BF16)
        _norm_into(x_ref, h_ref, HALO, ROW_TILE, gain, shift)
        up(0)
        o_ref[...] = jnp.zeros_like(o_ref)

    @pl.when(j == 1)
    def _():
        up(1)
        for piece in range(FFN_PIECES):
            gate(0, piece)

    for slot in range(2):
        @pl.when((j >= 2) & (j < N_FF) & (lax.rem(j, 2) == slot))
        def _():
            gate_and_down(1 - slot, slot)
            up(slot)

    @pl.when(j == N_FF)
    def _():
        gate_and_down(last_slot, 1 - last_slot)

    @pl.when(j == N_FF + 1)
    def _():
        for piece in range(FFN_PIECES):
            cols, acc = down(last_slot, piece)
            o_ref[:, cols] = x_ref[:, cols] + mod_ref[5:6, cols] * (o_ref[:, cols] + acc)


def _interleave_ff(t):
    lead = t.shape[:-1]
    return t.reshape(*lead, 2, N_FF, FF_BLOCK).swapaxes(-3, -2).reshape(*lead, 2 * D_FF)


def _ffn(x, mod, norm_w, w_up, conv_w, conv_b, w_down):
    tiles_per_seq = SEQ // ROW_TILE
    halo_blocks_per_tile = ROW_TILE // HALO
    blk = lambda j, lag: jnp.clip(j - lag, 0, N_FF - 1)
    ext_rows = HALO + ROW_TILE
    scratch = ext_rows * D_MODEL * 2 + 2 * ext_rows * 2 * FF_BLOCK * 4 + 2 * ROW_TILE * FF_BLOCK * 2
    blocks = 4 * ROW_TILE * D_MODEL * 4 + 6 * D_MODEL * FF_BLOCK * 2 + scratch + 6 * ext_rows * FF_BLOCK * 4
    return pl.pallas_call(
        _ffn_kernel,
        grid=(TOKENS // ROW_TILE, N_FF + 2),
        in_specs=[
            pl.BlockSpec((ROW_TILE, D_MODEL), lambda i, j: (i, 0)),
            pl.BlockSpec((HALO, D_MODEL), lambda i, j: (jnp.maximum(i * halo_blocks_per_tile - 1, 0), 0)),
            pl.BlockSpec((None, N_MOD, D_MODEL), lambda i, j: (i // tiles_per_seq, 0, 0)),
            pl.BlockSpec((1, D_MODEL), lambda i, j: (0, 0)),
            pl.BlockSpec((D_MODEL, 2 * FF_BLOCK), lambda i, j: (0, blk(j, 0))),
            pl.BlockSpec((CONV_WIDTH, 2 * FF_BLOCK), lambda i, j: (0, blk(j, 1))),
            pl.BlockSpec((1, 2 * FF_BLOCK), lambda i, j: (0, blk(j, 1))),
            pl.BlockSpec((FF_BLOCK, D_MODEL), lambda i, j: (blk(j, 2), 0)),
        ],
        out_specs=pl.BlockSpec((ROW_TILE, D_MODEL), lambda i, j: (i, 0)),
        out_shape=jax.ShapeDtypeStruct((TOKENS, D_MODEL), F32),
        scratch_shapes=[pltpu.VMEM((ext_rows, D_MODEL), BF16)] + 2 * [pltpu.VMEM((ext_rows, 2 * FF_BLOCK), F32)]
        + 2 * [pltpu.VMEM((ROW_TILE, FF_BLOCK), BF16)],
        compiler_params=_params(("arbitrary", "arbitrary"), blocks),
        name="ffn",
    )(x, x, mod, norm_w.reshape(1, D_MODEL), w_up, conv_w, conv_b.reshape(1, 2 * D_FF), w_down)


def kernel(x, c, w_ada, b_ada, ada_table, rel_bias, norm_mix_w, w_in, q_norm_w, k_norm_w, sinks, sgu_ln_w,
           sgu_ln_b, w_spatial, b_spatial, w_out, norm_ffn_w, w_up, conv_w, conv_b, w_down):
    B, S, D = x.shape
    assert (B, S, D) == (BATCH, SEQ, D_MODEL)
    mods = _ada_mod(c, w_ada, b_ada, ada_table).reshape(DEPTH, BATCH, N_MOD, D_MODEL)
    band_bias = _band_bias(rel_bias)
    xt = x.reshape(TOKENS, D_MODEL)
    for l in range(DEPTH):
        mod = mods[l]
        z = _mix_in(xt, mod, norm_mix_w[l], w_in[l].astype(BF16))
        y = _mixer(z, band_bias, sinks[l], q_norm_w[l], k_norm_w[l], sgu_ln_w[l], sgu_ln_b[l], w_spatial[l],
                   b_spatial[l])
        xt = _mix_out(y, w_out[l].astype(BF16), xt, mod)
        xt = _ffn(xt, mod, norm_ffn_w[l], _interleave_ff(w_up[l]).astype(BF16), _interleave_ff(conv_w[l]),
                  _interleave_ff(conv_b[l]), w_down[l].astype(BF16))
    return xt.reshape(B, S, D)
```

```python
import jax
import jax.numpy as jnp
import math
from jax import lax
from jax.experimental import pallas as pl
from jax.experimental.pallas import tpu as pltpu

D_MODEL = 4096
BATCH = 8
SEQ = 2048
DEPTH = 4
HEAD_DIM = 64
N_Q_HEADS = (D_MODEL // 2) // HEAD_DIM
N_KV_HEADS = 8
GQA_GROUP = N_Q_HEADS // N_KV_HEADS
ATTN_WIDTH = N_Q_HEADS * HEAD_DIM
KV_WIDTH = N_KV_HEADS * HEAD_DIM
WINDOW = 128
BLOCK = WINDOW
SGU_WIDTH = D_MODEL - ATTN_WIDTH
SGU_GROUPS = 8
SGU_GROUP_DIM = SGU_WIDTH // SGU_GROUPS
CHUNK = 128
MIX_WIDTH = ATTN_WIDTH + SGU_WIDTH
IN_WIDTH = ATTN_WIDTH + 2 * KV_WIDTH + 2 * SGU_WIDTH
D_FF = 11008
CONV_WIDTH = 3
N_BUCKETS = 32
MAX_DISTANCE = 128
N_MOD = 6
EPS = 1e-6
NEG_INF = -1e30

TOKENS = BATCH * SEQ
F32 = jnp.float32
BF16 = jnp.bfloat16

V7X_VMEM_BYTES = 64 * 2**20
V7X_LANES = 128
V7X_MXU_DIM = 256
BF16_SUBLANES = 16
MIB = 2**20

ROW_TILE = 512
NORM_CHUNK = 16
NORM_UNROLL = 4
MIX_IN_COLS = 1024
MIX_OUT_ROWS = 1024
MIX_OUT_COLS = 512
FF_BLOCK = 256
N_FF = D_FF // FF_BLOCK
HALO = BF16_SUBLANES
GATE_ROWS = 32
FFN_PIECES = 8
ADA_COLS = 512
CAST_ROWS = 1024
SGU_HALF = SGU_WIDTH // 2
PAIR = 2 * HEAD_DIM
QGROUP = GQA_GROUP * HEAD_DIM

assert PAIR == V7X_LANES and QGROUP == V7X_MXU_DIM and GQA_GROUP == 4
assert D_FF % FF_BLOCK == 0 and SEQ % ROW_TILE == 0 and SEQ % MIX_OUT_ROWS == 0


def _params(semantics, vmem_bytes):
    return pltpu.CompilerParams(dimension_semantics=semantics,
                                vmem_limit_bytes=int(min(V7X_VMEM_BYTES - 6 * MIB, vmem_bytes)))


def _ada_kernel(c_ref, w_ref, b_ref, tab_ref, o_ref):
    c = c_ref[...]
    a = c * jax.nn.sigmoid(c)
    mod = jnp.dot(a, w_ref[...], preferred_element_type=F32, precision=lax.Precision.HIGHEST) + b_ref[...]
    for l in range(DEPTH):
        o_ref[l] = mod + tab_ref[l:l + 1, :]


def _ada_mod(c, w_ada, b_ada, ada_table):
    width = N_MOD * D_MODEL
    return pl.pallas_call(
        _ada_kernel,
        grid=(width // ADA_COLS,),
        in_specs=[
            pl.BlockSpec((BATCH, D_MODEL), lambda j: (0, 0)),
            pl.BlockSpec((D_MODEL, ADA_COLS), lambda j: (0, j)),
            pl.BlockSpec((1, ADA_COLS), lambda j: (0, j)),
            pl.BlockSpec((DEPTH, ADA_COLS), lambda j: (0, j)),
        ],
        out_specs=pl.BlockSpec((DEPTH, BATCH, ADA_COLS), lambda j: (0, 0, j)),
        out_shape=jax.ShapeDtypeStruct((DEPTH, BATCH, width), F32),
        compiler_params=_params(("arbitrary",), 4 * D_MODEL * ADA_COLS * 4 + 8 * MIB),
        name="ada_mod",
    )(c, w_ada, b_ada.reshape(1, width), ada_table.reshape(DEPTH, width))


def _t5_causal_bucket(dist):
    max_exact = N_BUCKETS // 2
    n = jnp.maximum(dist, 0)
    nf = jnp.maximum(n, 1).astype(jnp.float32)
    large = max_exact + (jnp.log(nf / max_exact) / math.log(MAX_DISTANCE / max_exact)
                         * (N_BUCKETS - max_exact)).astype(jnp.int32)
    large = jnp.minimum(large, N_BUCKETS - 1)
    return jnp.where(n < max_exact, n, large)


def _bias_kernel(rb_ref, bucket_ref, o_ref):
    hk = pl.program_id(1)
    bucket = bucket_ref[...]
    for r in range(GQA_GROUP // 2):
        for par in range(2):
            hq = hk * GQA_GROUP + 2 * r + par
            acc = jnp.full(bucket.shape, NEG_INF, F32)
            for b in range(N_BUCKETS):
                acc = jnp.where(bucket == b, rb_ref[b, hq], acc)
            o_ref[r * BLOCK:(r + 1) * BLOCK, par * 2 * BLOCK:(par + 1) * 2 * BLOCK] = acc


def _band_bias(rel_bias):
    qi = jnp.arange(BLOCK)[:, None]
    kj = jnp.arange(2 * BLOCK)[None, :]
    dist = qi + BLOCK - kj
    in_window = (dist >= 0) & (dist < WINDOW)
    bucket = jnp.where(in_window, _t5_causal_bucket(dist), -1).astype(jnp.int32)
    buckets = jnp.stack([jnp.where(kj >= BLOCK, bucket, -1), bucket])
    return pl.pallas_call(
        _bias_kernel,
        grid=(2, N_KV_HEADS),
        in_specs=[
            pl.BlockSpec(memory_space=pltpu.SMEM),
            pl.BlockSpec((None, BLOCK, 2 * BLOCK), lambda v, h: (v, 0, 0)),
        ],
        out_specs=pl.BlockSpec((None, None, 2 * BLOCK, 4 * BLOCK), lambda v, h: (v, h, 0, 0)),
        out_shape=jax.ShapeDtypeStruct((2, N_KV_HEADS, 2 * BLOCK, 4 * BLOCK), F32),
        name="band_bias",
    )(rel_bias, buckets)


def _adaln_rows(x, gain, shift):
    ms = jnp.mean(x * x, axis=-1, keepdims=True)
    return x * lax.rsqrt(ms + EPS) * gain + shift


def _norm_into(x_ref, h_ref, h_row0, rows, gain, shift):
    def body(r, carry):
        src = pl.multiple_of(r * NORM_CHUNK, NORM_CHUNK)
        h = _adaln_rows(x_ref[pl.ds(src, NORM_CHUNK), :], gain, shift)
        h_ref[pl.ds(pl.multiple_of(h_row0 + src, NORM_CHUNK), NORM_CHUNK), :] = h.astype(BF16)
        return carry

    lax.fori_loop(0, rows // NORM_CHUNK, body, 0, unroll=NORM_UNROLL)


def _mix_in_kernel(x_ref, mod_ref, nw_ref, w_ref, o_ref, h_ref):
    @pl.when(pl.program_id(1) == 0)
    def _():
        _norm_into(x_ref, h_ref, 0, ROW_TILE, nw_ref[...] * (1.0 + mod_ref[1:2, :]), mod_ref[0:1, :])

    o_ref[...] = jnp.dot(h_ref[...], w_ref[...], preferred_element_type=F32).astype(o_ref.dtype)


def _mix_in(x, mod, norm_w, w_in):
    tiles_per_seq = SEQ // ROW_TILE
    blocks = (2 * ROW_TILE * D_MODEL * 4 + ROW_TILE * D_MODEL * 2
              + 2 * D_MODEL * MIX_IN_COLS * 2 + 4 * ROW_TILE * MIX_IN_COLS * 4)
    return pl.pallas_call(
        _mix_in_kernel,
        grid=(TOKENS // ROW_TILE, IN_WIDTH // MIX_IN_COLS),
        in_specs=[
            pl.BlockSpec((ROW_TILE, D_MODEL), lambda i, j: (i, 0)),
            pl.BlockSpec((None, N_MOD, D_MODEL), lambda i, j: (i // tiles_per_seq, 0, 0)),
            pl.BlockSpec((1, D_MODEL), lambda i, j: (0, 0)),
            pl.BlockSpec((None, D_MODEL, MIX_IN_COLS), lambda i, j: (j, 0, 0)),
        ],
        out_specs=pl.BlockSpec((ROW_TILE, MIX_IN_COLS), lambda i, j: (i, j)),
        out_shape=jax.ShapeDtypeStruct((TOKENS, IN_WIDTH), BF16),
        scratch_shapes=[pltpu.VMEM((ROW_TILE, D_MODEL), BF16)],
        compiler_params=_params(("arbitrary", "arbitrary"), blocks + 4 * MIB),
        name="mix_in",
    )(x, mod, norm_w.reshape(1, D_MODEL), w_in)


Z_K_BLOCK = ATTN_WIDTH // KV_WIDTH
Z_V_BLOCK = Z_K_BLOCK + 1
Z_U_BLOCK = (ATTN_WIDTH + 2 * KV_WIDTH) // SGU_HALF
Z_VS_BLOCK = Z_U_BLOCK + 2


def _grouped_rms(t, gain, head_ones):
    rows, width = t.shape
    n = width // QGROUP
    sq = t * t
    stacked = jnp.concatenate([sq[:, c * QGROUP:(c + 1) * QGROUP] for c in range(n)], axis=0).astype(BF16)
    ss = jnp.dot(stacked, head_ones, preferred_element_type=F32)
    rs = lax.rsqrt(ss * (1.0 / HEAD_DIM) + EPS)
    return [t[:, c * QGROUP:(c + 1) * QGROUP] * rs[c * rows:(c + 1) * rows] * gain for c in range(n)]


def _attention(q_ref, kc_ref, kp_ref, vc_ref, vp_ref, bias_ref, sinks_ref, qg_ref, kg_ref, ones_ref, o_ref):
    rows2 = 2 * BLOCK
    head_ones = ones_ref[...]
    lo = lax.broadcasted_iota(jnp.int32, (rows2, PAIR), 1) < HEAD_DIM
    top = lax.broadcasted_iota(jnp.int32, (rows2, 1), 0) < BLOCK
    ones_cols = jnp.concatenate([jnp.where(lo, 1.0, 0.0), jnp.where(lo, 0.0, 1.0)], axis=0).astype(BF16)

    qn = _grouped_rms(q_ref[...].astype(F32), qg_ref[...], head_ones)
    kn = _grouped_rms(jnp.concatenate([kp_ref[...], kc_ref[...]], axis=0).astype(F32), kg_ref[...], head_ones)
    vf = jnp.concatenate([vp_ref[...], vc_ref[...]], axis=0).astype(F32)

    def split_pair(pair):
        swapped = pltpu.roll(pair, HEAD_DIM, axis=1)
        a = jnp.concatenate([jnp.where(lo, pair, 0.0), jnp.where(lo, 0.0, swapped)], axis=0)
        b = jnp.concatenate([jnp.where(lo, swapped, 0.0), jnp.where(lo, 0.0, pair)], axis=0)
        return a.astype(BF16), b.astype(BF16)

    kk, vv = [], []
    for p in range(N_KV_HEADS // 2):
        kk.extend(split_pair(kn[p // 2][:, (p % 2) * PAIR:(p % 2 + 1) * PAIR]))
        vv.extend(split_pair(vf[:, p * PAIR:(p + 1) * PAIR]))

    def scores(hk):
        q2 = jnp.concatenate([qn[hk][:, :PAIR], qn[hk][:, PAIR:]], axis=0).astype(BF16)
        s = lax.dot_general(q2, kk[hk], (((1,), (1,)), ((), ())), preferred_element_type=F32)
        return s + bias_ref[hk]

    def softmax_pv(hk, s):
        probs, sink_terms = [], []
        for par in range(2):
            sp = s[:, par * rows2:(par + 1) * rows2]
            sink = jnp.where(top, sinks_ref[hk * GQA_GROUP + par], sinks_ref[hk * GQA_GROUP + 2 + par])
            m = jnp.maximum(jnp.max(sp, axis=-1, keepdims=True), sink)
            probs.append(jnp.exp(sp - m).astype(BF16))
            sink_terms.append(jnp.exp(sink - m))
        nd = jnp.dot(jnp.concatenate(probs, axis=1), jnp.concatenate([vv[hk], ones_cols], axis=1),
                     preferred_element_type=F32)
        den = nd[:, PAIR:] + jnp.where(lo, sink_terms[0], sink_terms[1])
        out = (nd[:, :PAIR] / den).astype(o_ref.dtype)
        o_ref[:, (2 * hk) * PAIR:(2 * hk + 1) * PAIR] = out[:BLOCK]
        o_ref[:, (2 * hk + 1) * PAIR:(2 * hk + 2) * PAIR] = out[BLOCK:]

    s_next = scores(0)
    for hk in range(N_KV_HEADS):
        s = s_next
        if hk + 1 < N_KV_HEADS:
            s_next = scores(hk + 1)
        softmax_pv(hk, s)


def _gelu(t):
    return 0.5 * t * (1.0 + lax.erf(t * (2.0 ** -0.5)))


def _sgu(u_refs, vs_refs, lnw_ref, lnb_ref, ws_ref, bst_ref, o_ref):
    v = _gelu(jnp.concatenate([r[...] for r in vs_refs], axis=1).astype(F32))
    mu = jnp.mean(v, axis=-1, keepdims=True)
    vc = v - mu
    var = jnp.mean(vc * vc, axis=-1, keepdims=True)
    vn = (vc * lax.rsqrt(var + EPS) * lnw_ref[...] + lnb_ref[...]).astype(BF16)
    row = lax.broadcasted_iota(jnp.int32, (CHUNK, CHUNK), 0)
    col = lax.broadcasted_iota(jnp.int32, (CHUNK, CHUNK), 1)
    causal = col <= row
    groups_per_ref = SGU_HALF // SGU_GROUP_DIM
    for g in range(SGU_GROUPS):
        ws = jnp.where(causal, ws_ref[g], 0.0).astype(BF16)
        mixed = (jnp.dot(ws, vn[:, g * SGU_GROUP_DIM:(g + 1) * SGU_GROUP_DIM], preferred_element_type=F32)
                 + bst_ref[:, g:g + 1])
        gr = g % groups_per_ref
        u = _gelu(u_refs[g // groups_per_ref][:, gr * SGU_GROUP_DIM:(gr + 1) * SGU_GROUP_DIM].astype(F32))
        o_ref[:, ATTN_WIDTH + g * SGU_GROUP_DIM:ATTN_WIDTH + (g + 1) * SGU_GROUP_DIM] = (
            u * mixed).astype(o_ref.dtype)


def _mixer_kernel(q_ref, kc_ref, kp_ref, vc_ref, vp_ref, u0_ref, u1_ref, vs0_ref, vs1_ref, bias_ref, sinks_ref,
                  qg_ref, kg_ref, ones_ref, lnw_ref, lnb_ref, ws_ref, bst_ref, o_ref):
    _attention(q_ref, kc_ref, kp_ref, vc_ref, vp_ref, bias_ref, sinks_ref, qg_ref, kg_ref, ones_ref, o_ref)
    _sgu((u0_ref, u1_ref), (vs0_ref, vs1_ref), lnw_ref, lnb_ref, ws_ref, bst_ref, o_ref)


def _mixer(z, band_bias, sinks, q_norm_w, k_norm_w, sgu_ln_w, sgu_ln_b, w_spatial, b_spatial):
    nb = SEQ // BLOCK
    cur = lambda b, n: b * nb + n
    prev = lambda b, n: b * nb + jnp.maximum(n - 1, 0)
    whole = lambda *shape: pl.BlockSpec(shape, lambda b, n: (0,) * len(shape))
    q_gain = jnp.tile(q_norm_w * (HEAD_DIM ** -0.5), GQA_GROUP).reshape(1, QGROUP)
    k_gain = jnp.tile(k_norm_w, GQA_GROUP).reshape(1, QGROUP)
    head_ones = jnp.kron(jnp.eye(GQA_GROUP, dtype=F32), jnp.ones((HEAD_DIM, HEAD_DIM), F32)).astype(BF16)
    return pl.pallas_call(
        _mixer_kernel,
        grid=(BATCH, nb),
        in_specs=[
            pl.BlockSpec((BLOCK, ATTN_WIDTH), lambda b, n: (cur(b, n), 0)),
            pl.BlockSpec((BLOCK, KV_WIDTH), lambda b, n: (cur(b, n), Z_K_BLOCK)),
            pl.BlockSpec((BLOCK, KV_WIDTH), lambda b, n: (prev(b, n), Z_K_BLOCK)),
            pl.BlockSpec((BLOCK, KV_WIDTH), lambda b, n: (cur(b, n), Z_V_BLOCK)),
            pl.BlockSpec((BLOCK, KV_WIDTH), lambda b, n: (prev(b, n), Z_V_BLOCK)),
            pl.BlockSpec((CHUNK, SGU_HALF), lambda b, n: (cur(b, n), Z_U_BLOCK)),
            pl.BlockSpec((CHUNK, SGU_HALF), lambda b, n: (cur(b, n), Z_U_BLOCK + 1)),
            pl.BlockSpec((CHUNK, SGU_HALF), lambda b, n: (cur(b, n), Z_VS_BLOCK)),
            pl.BlockSpec((CHUNK, SGU_HALF), lambda b, n: (cur(b, n), Z_VS_BLOCK + 1)),
            pl.BlockSpec((None, N_KV_HEADS, 2 * BLOCK, 4 * BLOCK), lambda b, n: (jnp.minimum(n, 1), 0, 0, 0)),
            pl.BlockSpec(memory_space=pltpu.SMEM),
            whole(1, QGROUP),
            whole(1, QGROUP),
            whole(QGROUP, QGROUP),
            whole(1, SGU_WIDTH),
            whole(1, SGU_WIDTH),
            whole(SGU_GROUPS, CHUNK, CHUNK),
            whole(CHUNK, SGU_GROUPS),
        ],
        out_specs=pl.BlockSpec((BLOCK, MIX_WIDTH), lambda b, n: (cur(b, n), 0)),
        out_shape=jax.ShapeDtypeStruct((TOKENS, MIX_WIDTH), BF16),
        compiler_params=_params(("arbitrary", "arbitrary"), 40 * MIB),
        name="mixer",
    )(z, z, z, z, z, z, z, z, z, band_bias, sinks, q_gain, k_gain, head_ones,
      sgu_ln_w.reshape(1, SGU_WIDTH), sgu_ln_b.reshape(1, SGU_WIDTH), w_spatial, b_spatial.T)


def _mix_out_kernel(y_ref, w_ref, x_ref, mod_ref, o_ref):
    acc = jnp.dot(y_ref[...], w_ref[...], preferred_element_type=F32)
    o_ref[...] = x_ref[...] + mod_ref[2:3, :] * acc


def _mix_out(y, w_out, x, mod):
    tiles_per_seq = SEQ // MIX_OUT_ROWS
    blocks = (2 * MIX_OUT_ROWS * MIX_WIDTH * 2 + 2 * MIX_WIDTH * MIX_OUT_COLS * 2
              + 6 * MIX_OUT_ROWS * MIX_OUT_COLS * 4)
    return pl.pallas_call(
        _mix_out_kernel,
        grid=(TOKENS // MIX_OUT_ROWS, D_MODEL // MIX_OUT_COLS),
        in_specs=[
            pl.BlockSpec((MIX_OUT_ROWS, MIX_WIDTH), lambda i, j: (i, 0)),
            pl.BlockSpec((None, MIX_WIDTH, MIX_OUT_COLS), lambda i, j: (j, 0, 0)),
            pl.BlockSpec((MIX_OUT_ROWS, MIX_OUT_COLS), lambda i, j: (i, j)),
            pl.BlockSpec((None, N_MOD, MIX_OUT_COLS), lambda i, j: (i // tiles_per_seq, 0, j)),
        ],
        out_specs=pl.BlockSpec((MIX_OUT_ROWS, MIX_OUT_COLS), lambda i, j: (i, j)),
        out_shape=jax.ShapeDtypeStruct((TOKENS, D_MODEL), F32),
        compiler_params=_params(("arbitrary", "arbitrary"), blocks + 4 * MIB),
        name="mix_out",
    )(y, w_out, x, mod)


def _ffn_kernel(x_ref, xh_ref, mod_ref, nw_ref, wg_ref, wv_ref, cwg_ref, cwv_ref, cbg_ref, cbv_ref, wd_ref, o_ref,
                h_ref, a0_ref, a1_ref, g0_ref, g1_ref):
    i = pl.program_id(0)
    j = pl.program_id(1)
    a_refs, g_refs = (a0_ref, a1_ref), (g0_ref, g1_ref)

    gate_cols, val_cols = slice(0, FF_BLOCK), slice(FF_BLOCK, 2 * FF_BLOCK)

    def up(slot):
        h = h_ref[...]
        a_refs[slot][:, gate_cols] = jnp.dot(h, wg_ref[...], preferred_element_type=F32)
        a_refs[slot][:, val_cols] = jnp.dot(h, wv_ref[...], preferred_element_type=F32)

    def gate(slot, piece):
        a_ref, g_ref = a_refs[slot], g_refs[slot]
        rows = ROW_TILE // FFN_PIECES

        def conv(r0, cols, cw_ref, cb_ref):
            lag = lambda k: a_ref[HALO - k + r0:HALO - k + r0 + GATE_ROWS, cols]
            return lag(2) * cw_ref[0:1, :] + lag(1) * cw_ref[1:2, :] + lag(0) * cw_ref[2:3, :] + cb_ref[...]

        for r0 in range(piece * rows, (piece + 1) * rows, GATE_ROWS):
            a_gate = conv(r0, gate_cols, cwg_ref, cbg_ref)
            a_val = conv(r0, val_cols, cwv_ref, cbv_ref)
            g_ref[r0:r0 + GATE_ROWS, :] = (a_gate * jax.nn.sigmoid(a_gate) * a_val).astype(BF16)

    def down(slot, piece):
        cols = slice(piece * (D_MODEL // FFN_PIECES), (piece + 1) * (D_MODEL // FFN_PIECES))
        return cols, jnp.dot(g_refs[slot][...], wd_ref[:, cols], preferred_element_type=F32)

    def gate_and_down(gate_slot, down_slot):
        for piece in range(FFN_PIECES):
            cols, acc = down(down_slot, piece)
            o_ref[:, cols] += acc
            gate(gate_slot, piece)

    last_slot = (N_FF - 1) % 2

    @pl.when(j == 0)
    def _():
        gain, shift = nw_ref[...] * (1.0 + mod_ref[4:5, :]), mod_ref[3:4, :]
        keep = jnp.where(i % (SEQ // ROW_TILE) == 0, 0.0, 1.0)
        h_ref[0:HALO, :] = (_adaln_rows(xh_ref[...], gain, shift) * keep).astype(BF16)
        _norm_into(x_ref, h_ref, HALO, ROW_TILE, gain, shift)
        up(0)
        o_ref[...] = jnp.zeros_like(o_ref)

    @pl.when(j == 1)
    def _():
        up(1)
        for piece in range(FFN_PIECES):
            gate(0, piece)

    for slot in range(2):
        @pl.when((j >= 2) & (j < N_FF) & (lax.rem(j, 2) == slot))
        def _():
            gate_and_down(1 - slot, slot)
            up(slot)

    @pl.when(j == N_FF)
    def _():
        gate_and_down(last_slot, 1 - last_slot)

    @pl.when(j == N_FF + 1)
    def _():
        for piece in range(FFN_PIECES):
            cols, acc = down(last_slot, piece)
            o_ref[:, cols] = x_ref[:, cols] + mod_ref[5:6, cols] * (o_ref[:, cols] + acc)


def _cast_kernel(w_ref, o_ref):
    o_ref[...] = w_ref[...].astype(o_ref.dtype)


def _column_blocks_bf16(w, layer, cols):
    _, rows, width = w.shape
    return pl.pallas_call(
        _cast_kernel,
        grid=(width // cols, rows // CAST_ROWS),
        in_specs=[pl.BlockSpec((None, CAST_ROWS, cols), lambda j, r: (layer, r, j))],
        out_specs=pl.BlockSpec((None, CAST_ROWS, cols), lambda j, r: (j, r, 0)),
        out_shape=jax.ShapeDtypeStruct((width // cols, rows, cols), BF16),
        compiler_params=_params(("arbitrary", "arbitrary"), 4 * CAST_ROWS * cols * 6),
        name="column_blocks_bf16",
    )(w)


def _ffn(x, mod, norm_w, w_up, conv_w, conv_b, w_down):
    tiles_per_seq = SEQ // ROW_TILE
    halo_blocks_per_tile = ROW_TILE // HALO
    conv_b = conv_b.reshape(1, 2 * D_FF)
    blk = lambda j, lag: jnp.clip(j - lag, 0, N_FF - 1)
    val = lambda b: N_FF + b
    ext_rows = HALO + ROW_TILE
    scratch = ext_rows * D_MODEL * 2 + 2 * ext_rows * 2 * FF_BLOCK * 4 + 2 * ROW_TILE * FF_BLOCK * 2
    blocks = 4 * ROW_TILE * D_MODEL * 4 + 6 * D_MODEL * FF_BLOCK * 2 + scratch + 6 * ext_rows * FF_BLOCK * 4
    return pl.pallas_call(
        _ffn_kernel,
        grid=(TOKENS // ROW_TILE, N_FF + 2),
        in_specs=[
            pl.BlockSpec((ROW_TILE, D_MODEL), lambda i, j: (i, 0)),
            pl.BlockSpec((HALO, D_MODEL), lambda i, j: (jnp.maximum(i * halo_blocks_per_tile - 1, 0), 0)),
            pl.BlockSpec((None, N_MOD, D_MODEL), lambda i, j: (i // tiles_per_seq, 0, 0)),
            pl.BlockSpec((1, D_MODEL), lambda i, j: (0, 0)),
            pl.BlockSpec((None, D_MODEL, FF_BLOCK), lambda i, j: (blk(j, 0), 0, 0)),
            pl.BlockSpec((None, D_MODEL, FF_BLOCK), lambda i, j: (val(blk(j, 0)), 0, 0)),
            pl.BlockSpec((CONV_WIDTH, FF_BLOCK), lambda i, j: (0, blk(j, 1))),
            pl.BlockSpec((CONV_WIDTH, FF_BLOCK), lambda i, j: (0, val(blk(j, 1)))),
            pl.BlockSpec((1, FF_BLOCK), lambda i, j: (0, blk(j, 1))),
            pl.BlockSpec((1, FF_BLOCK), lambda i, j: (0, val(blk(j, 1)))),
            pl.BlockSpec((FF_BLOCK, D_MODEL), lambda i, j: (blk(j, 2), 0)),
        ],
        out_specs=pl.BlockSpec((ROW_TILE, D_MODEL), lambda i, j: (i, 0)),
        out_shape=jax.ShapeDtypeStruct((TOKENS, D_MODEL), F32),
        scratch_shapes=[pltpu.VMEM((ext_rows, D_MODEL), BF16)] + 2 * [pltpu.VMEM((ext_rows, 2 * FF_BLOCK), F32)]
        + 2 * [pltpu.VMEM((ROW_TILE, FF_BLOCK), BF16)],
        compiler_params=_params(("arbitrary", "arbitrary"), blocks),
        name="ffn",
    )(x, x, mod, norm_w.reshape(1, D_MODEL), w_up, w_up, conv_w, conv_w, conv_b, conv_b, w_down)


def kernel(x, c, w_ada, b_ada, ada_table, rel_bias, norm_mix_w, w_in, q_norm_w, k_norm_w, sinks, sgu_ln_w,
           sgu_ln_b, w_spatial, b_spatial, w_out, norm_ffn_w, w_up, conv_w, conv_b, w_down):
    B, S, D = x.shape
    assert (B, S, D) == (BATCH, SEQ, D_MODEL)
    mods = _ada_mod(c, w_ada, b_ada, ada_table).reshape(DEPTH, BATCH, N_MOD, D_MODEL)
    band_bias = _band_bias(rel_bias)
    xt = x.reshape(TOKENS, D_MODEL)
    for l in range(DEPTH):
        mod = mods[l]
        z = _mix_in(xt, mod, norm_mix_w[l], _column_blocks_bf16(w_in, l, MIX_IN_COLS))
        y = _mixer(z, band_bias, sinks[l], q_norm_w[l], k_norm_w[l], sgu_ln_w[l], sgu_ln_b[l], w_spatial[l],
                   b_spatial[l])
        xt = _mix_out(y, _column_blocks_bf16(w_out, l, MIX_OUT_COLS), xt, mod)
        xt = _ffn(xt, mod, norm_ffn_w[l], _column_blocks_bf16(w_up, l, FF_BLOCK), conv_w[l], conv_b[l],
                  w_down[l].astype(BF16))
    return xt.reshape(B, S, D)
```

```python
import jax
import jax.numpy as jnp
import math
from jax import lax
from jax.experimental import pallas as pl
from jax.experimental.pallas import tpu as pltpu

D_MODEL = 4096
BATCH = 8
SEQ = 2048
DEPTH = 4
HEAD_DIM = 64
N_Q_HEADS = (D_MODEL // 2) // HEAD_DIM
N_KV_HEADS = 8
GQA_GROUP = N_Q_HEADS // N_KV_HEADS
ATTN_WIDTH = N_Q_HEADS * HEAD_DIM
KV_WIDTH = N_KV_HEADS * HEAD_DIM
WINDOW = 128
BLOCK = WINDOW
SGU_WIDTH = D_MODEL - ATTN_WIDTH
SGU_GROUPS = 8
SGU_GROUP_DIM = SGU_WIDTH // SGU_GROUPS
CHUNK = 128
MIX_WIDTH = ATTN_WIDTH + SGU_WIDTH
IN_WIDTH = ATTN_WIDTH + 2 * KV_WIDTH + 2 * SGU_WIDTH
D_FF = 11008
CONV_WIDTH = 3
N_BUCKETS = 32
MAX_DISTANCE = 128
N_MOD = 6
EPS = 1e-6
NEG_INF = -1e30

TOKENS = BATCH * SEQ
F32 = jnp.float32
BF16 = jnp.bfloat16

V7X_VMEM_BYTES = 64 * 2**20
V7X_LANES = 128
V7X_MXU_DIM = 256
BF16_SUBLANES = 16
MIB = 2**20

ROW_TILE = 512
NORM_CHUNK = 16
NORM_UNROLL = 4
MIX_IN_COLS = 1024
MIX_OUT_ROWS = 1024
MIX_OUT_COLS = 512
FF_BLOCK = 256
N_FF = D_FF // FF_BLOCK
HALO = BF16_SUBLANES
GATE_ROWS = 32
FFN_PIECES = 8
ADA_COLS = 512
SGU_HALF = SGU_WIDTH // 2
PAIR = 2 * HEAD_DIM
QGROUP = GQA_GROUP * HEAD_DIM

assert PAIR == V7X_LANES and QGROUP == V7X_MXU_DIM and GQA_GROUP == 4
assert D_FF % FF_BLOCK == 0 and SEQ % ROW_TILE == 0 and SEQ % MIX_OUT_ROWS == 0


def _params(semantics, vmem_bytes):
    return pltpu.CompilerParams(dimension_semantics=semantics,
                                vmem_limit_bytes=int(min(V7X_VMEM_BYTES - 6 * MIB, vmem_bytes)))


def _ada_kernel(c_ref, w_ref, b_ref, tab_ref, o_ref):
    c = c_ref[...]
    a = c * jax.nn.sigmoid(c)
    mod = jnp.dot(a, w_ref[...], preferred_element_type=F32, precision=lax.Precision.HIGHEST) + b_ref[...]
    for l in range(DEPTH):
        o_ref[l] = mod + tab_ref[l:l + 1, :]


def _ada_mod(c, w_ada, b_ada, ada_table):
    width = N_MOD * D_MODEL
    return pl.pallas_call(
        _ada_kernel,
        grid=(width // ADA_COLS,),
        in_specs=[
            pl.BlockSpec((BATCH, D_MODEL), lambda j: (0, 0)),
            pl.BlockSpec((D_MODEL, ADA_COLS), lambda j: (0, j)),
            pl.BlockSpec((1, ADA_COLS), lambda j: (0, j)),
            pl.BlockSpec((DEPTH, ADA_COLS), lambda j: (0, j)),
        ],
        out_specs=pl.BlockSpec((DEPTH, BATCH, ADA_COLS), lambda j: (0, 0, j)),
        out_shape=jax.ShapeDtypeStruct((DEPTH, BATCH, width), F32),
        compiler_params=_params(("arbitrary",), 4 * D_MODEL * ADA_COLS * 4 + 8 * MIB),
        name="ada_mod",
    )(c, w_ada, b_ada.reshape(1, width), ada_table.reshape(DEPTH, width))


def _t5_causal_bucket(dist):
    max_exact = N_BUCKETS // 2
    n = jnp.maximum(dist, 0)
    nf = jnp.maximum(n, 1).astype(jnp.float32)
    large = max_exact + (jnp.log(nf / max_exact) / math.log(MAX_DISTANCE / max_exact)
                         * (N_BUCKETS - max_exact)).astype(jnp.int32)
    large = jnp.minimum(large, N_BUCKETS - 1)
    return jnp.where(n < max_exact, n, large)


def _bias_kernel(rb_ref, bucket_ref, o_ref):
    hk = pl.program_id(1)
    bucket = bucket_ref[...]
    for r in range(GQA_GROUP // 2):
        for par in range(2):
            hq = hk * GQA_GROUP + 2 * r + par
            acc = jnp.full(bucket.shape, NEG_INF, F32)
            for b in range(N_BUCKETS):
                acc = jnp.where(bucket == b, rb_ref[b, hq], acc)
            o_ref[r * BLOCK:(r + 1) * BLOCK, par * 2 * BLOCK:(par + 1) * 2 * BLOCK] = acc


def _band_bias(rel_bias):
    qi = jnp.arange(BLOCK)[:, None]
    kj = jnp.arange(2 * BLOCK)[None, :]
    dist = qi + BLOCK - kj
    in_window = (dist >= 0) & (dist < WINDOW)
    bucket = jnp.where(in_window, _t5_causal_bucket(dist), -1).astype(jnp.int32)
    buckets = jnp.stack([jnp.where(kj >= BLOCK, bucket, -1), bucket])
    return pl.pallas_call(
        _bias_kernel,
        grid=(2, N_KV_HEADS),
        in_specs=[
            pl.BlockSpec(memory_space=pltpu.SMEM),
            pl.BlockSpec((None, BLOCK, 2 * BLOCK), lambda v, h: (v, 0, 0)),
        ],
        out_specs=pl.BlockSpec((None, None, 2 * BLOCK, 4 * BLOCK), lambda v, h: (v, h, 0, 0)),
        out_shape=jax.ShapeDtypeStruct((2, N_KV_HEADS, 2 * BLOCK, 4 * BLOCK), F32),
        name="band_bias",
    )(rel_bias, buckets)


def _adaln_rows(x, gain, shift):
    ms = jnp.mean(x * x, axis=-1, keepdims=True)
    return x * lax.rsqrt(ms + EPS) * gain + shift


def _norm_into(x_ref, h_ref, h_row0, rows, gain, shift):
    def body(r, carry):
        src = pl.multiple_of(r * NORM_CHUNK, NORM_CHUNK)
        h = _adaln_rows(x_ref[pl.ds(src, NORM_CHUNK), :], gain, shift)
        h_ref[pl.ds(pl.multiple_of(h_row0 + src, NORM_CHUNK), NORM_CHUNK), :] = h.astype(BF16)
        return carry

    lax.fori_loop(0, rows // NORM_CHUNK, body, 0, unroll=NORM_UNROLL)


def _mix_in_kernel(x_ref, mod_ref, nw_ref, w_ref, o_ref, h_ref):
    @pl.when(pl.program_id(1) == 0)
    def _():
        _norm_into(x_ref, h_ref, 0, ROW_TILE, nw_ref[...] * (1.0 + mod_ref[1:2, :]), mod_ref[0:1, :])

    o_ref[...] = jnp.dot(h_ref[...], w_ref[...], preferred_element_type=F32).astype(o_ref.dtype)


def _mix_in(x, mod, norm_w, w_in, layer):
    tiles_per_seq = SEQ // ROW_TILE
    blocks = (2 * ROW_TILE * D_MODEL * 4 + ROW_TILE * D_MODEL * 2
              + 2 * D_MODEL * MIX_IN_COLS * 2 + 4 * ROW_TILE * MIX_IN_COLS * 4)
    return pl.pallas_call(
        _mix_in_kernel,
        grid=(TOKENS // ROW_TILE, IN_WIDTH // MIX_IN_COLS),
        in_specs=[
            pl.BlockSpec((ROW_TILE, D_MODEL), lambda i, j: (i, 0)),
            pl.BlockSpec((None, N_MOD, D_MODEL), lambda i, j: (i // tiles_per_seq, 0, 0)),
            pl.BlockSpec((1, D_MODEL), lambda i, j: (0, 0)),
            pl.BlockSpec((None, D_MODEL, MIX_IN_COLS), lambda i, j: (layer, 0, j)),
        ],
        out_specs=pl.BlockSpec((ROW_TILE, MIX_IN_COLS), lambda i, j: (i, j)),
        out_shape=jax.ShapeDtypeStruct((TOKENS, IN_WIDTH), BF16),
        scratch_shapes=[pltpu.VMEM((ROW_TILE, D_MODEL), BF16)],
        compiler_params=_params(("arbitrary", "arbitrary"), blocks + 4 * MIB),
        name="mix_in",
    )(x, mod, norm_w.reshape(1, D_MODEL), w_in)


Z_K_BLOCK = ATTN_WIDTH // KV_WIDTH
Z_V_BLOCK = Z_K_BLOCK + 1
Z_U_BLOCK = (ATTN_WIDTH + 2 * KV_WIDTH) // SGU_HALF
Z_VS_BLOCK = Z_U_BLOCK + 2


def _grouped_rms(t, gain, head_ones):
    rows, width = t.shape
    n = width // QGROUP
    sq = t * t
    stacked = jnp.concatenate([sq[:, c * QGROUP:(c + 1) * QGROUP] for c in range(n)], axis=0).astype(BF16)
    ss = jnp.dot(stacked, head_ones, preferred_element_type=F32)
    rs = lax.rsqrt(ss * (1.0 / HEAD_DIM) + EPS)
    return [t[:, c * QGROUP:(c + 1) * QGROUP] * rs[c * rows:(c + 1) * rows] * gain for c in range(n)]


def _attention(q_ref, kc_ref, kp_ref, vc_ref, vp_ref, bias_ref, sinks_ref, qg_ref, kg_ref, ones_ref, o_ref):
    rows2 = 2 * BLOCK
    head_ones = ones_ref[...]
    lo = lax.broadcasted_iota(jnp.int32, (rows2, PAIR), 1) < HEAD_DIM
    top = lax.broadcasted_iota(jnp.int32, (rows2, 1), 0) < BLOCK
    ones_cols = jnp.concatenate([jnp.where(lo, 1.0, 0.0), jnp.where(lo, 0.0, 1.0)], axis=0).astype(BF16)

    qn = _grouped_rms(q_ref[...].astype(F32), qg_ref[...], head_ones)
    kn = _grouped_rms(jnp.concatenate([kp_ref[...], kc_ref[...]], axis=0).astype(F32), kg_ref[...], head_ones)
    vf = jnp.concatenate([vp_ref[...], vc_ref[...]], axis=0).astype(F32)

    def split_pair(pair):
        swapped = pltpu.roll(pair, HEAD_DIM, axis=1)
        a = jnp.concatenate([jnp.where(lo, pair, 0.0), jnp.where(lo, 0.0, swapped)], axis=0)
        b = jnp.concatenate([jnp.where(lo, swapped, 0.0), jnp.where(lo, 0.0, pair)], axis=0)
        return a.astype(BF16), b.astype(BF16)

    kk, vv = [], []
    for p in range(N_KV_HEADS // 2):
        kk.extend(split_pair(kn[p // 2][:, (p % 2) * PAIR:(p % 2 + 1) * PAIR]))
        vv.extend(split_pair(vf[:, p * PAIR:(p + 1) * PAIR]))

    def scores(hk):
        q2 = jnp.concatenate([qn[hk][:, :PAIR], qn[hk][:, PAIR:]], axis=0).astype(BF16)
        s = lax.dot_general(q2, kk[hk], (((1,), (1,)), ((), ())), preferred_element_type=F32)
        return s + bias_ref[hk]

    def softmax_pv(hk, s):
        probs, sink_terms = [], []
        for par in range(2):
            sp = s[:, par * rows2:(par + 1) * rows2]
            sink = jnp.where(top, sinks_ref[hk * GQA_GROUP + par], sinks_ref[hk * GQA_GROUP + 2 + par])
            m = jnp.maximum(jnp.max(sp, axis=-1, keepdims=True), sink)
            probs.append(jnp.exp(sp - m).astype(BF16))
            sink_terms.append(jnp.exp(sink - m))
        nd = jnp.dot(jnp.concatenate(probs, axis=1), jnp.concatenate([vv[hk], ones_cols], axis=1),
                     preferred_element_type=F32)
        den = nd[:, PAIR:] + jnp.where(lo, sink_terms[0], sink_terms[1])
        out = (nd[:, :PAIR] / den).astype(o_ref.dtype)
        o_ref[:, (2 * hk) * PAIR:(2 * hk + 1) * PAIR] = out[:BLOCK]
        o_ref[:, (2 * hk + 1) * PAIR:(2 * hk + 2) * PAIR] = out[BLOCK:]

    s_next = scores(0)
    for hk in range(N_KV_HEADS):
        s = s_next
        if hk + 1 < N_KV_HEADS:
            s_next = scores(hk + 1)
        softmax_pv(hk, s)


def _gelu(t):
    return 0.5 * t * (1.0 + lax.erf(t * (2.0 ** -0.5)))


def _sgu(u_refs, vs_refs, lnw_ref, lnb_ref, ws_ref, bst_ref, o_ref):
    v = _gelu(jnp.concatenate([r[...] for r in vs_refs], axis=1).astype(F32))
    mu = jnp.mean(v, axis=-1, keepdims=True)
    vc = v - mu
    var = jnp.mean(vc * vc, axis=-1, keepdims=True)
    vn = (vc * lax.rsqrt(var + EPS) * lnw_ref[...] + lnb_ref[...]).astype(BF16)
    row = lax.broadcasted_iota(jnp.int32, (CHUNK, CHUNK), 0)
    col = lax.broadcasted_iota(jnp.int32, (CHUNK, CHUNK), 1)
    causal = col <= row
    groups_per_ref = SGU_HALF // SGU_GROUP_DIM
    for g in range(SGU_GROUPS):
        ws = jnp.where(causal, ws_ref[g], 0.0).astype(BF16)
        mixed = (jnp.dot(ws, vn[:, g * SGU_GROUP_DIM:(g + 1) * SGU_GROUP_DIM], preferred_element_type=F32)
                 + bst_ref[:, g:g + 1])
        gr = g % groups_per_ref
        u = _gelu(u_refs[g // groups_per_ref][:, gr * SGU_GROUP_DIM:(gr + 1) * SGU_GROUP_DIM].astype(F32))
        o_ref[:, ATTN_WIDTH + g * SGU_GROUP_DIM:ATTN_WIDTH + (g + 1) * SGU_GROUP_DIM] = (
            u * mixed).astype(o_ref.dtype)


def _mixer_kernel(q_ref, kc_ref, kp_ref, vc_ref, vp_ref, u0_ref, u1_ref, vs0_ref, vs1_ref, bias_ref, sinks_ref,
                  qg_ref, kg_ref, ones_ref, lnw_ref, lnb_ref, ws_ref, bst_ref, o_ref):
    _attention(q_ref, kc_ref, kp_ref, vc_ref, vp_ref, bias_ref, sinks_ref, qg_ref, kg_ref, ones_ref, o_ref)
    _sgu((u0_ref, u1_ref), (vs0_ref, vs1_ref), lnw_ref, lnb_ref, ws_ref, bst_ref, o_ref)


def _mixer(z, band_bias, sinks, q_norm_w, k_norm_w, sgu_ln_w, sgu_ln_b, w_spatial, b_spatial):
    nb = SEQ // BLOCK
    cur = lambda b, n: b * nb + n
    prev = lambda b, n: b * nb + jnp.maximum(n - 1, 0)
    whole = lambda *shape: pl.BlockSpec(shape, lambda b, n: (0,) * len(shape))
    q_gain = jnp.tile(q_norm_w * (HEAD_DIM ** -0.5), GQA_GROUP).reshape(1, QGROUP)
    k_gain = jnp.tile(k_norm_w, GQA_GROUP).reshape(1, QGROUP)
    head_ones = jnp.kron(jnp.eye(GQA_GROUP, dtype=F32), jnp.ones((HEAD_DIM, HEAD_DIM), F32)).astype(BF16)
    return pl.pallas_call(
        _mixer_kernel,
        grid=(BATCH, nb),
        in_specs=[
            pl.BlockSpec((BLOCK, ATTN_WIDTH), lambda b, n: (cur(b, n), 0)),
            pl.BlockSpec((BLOCK, KV_WIDTH), lambda b, n: (cur(b, n), Z_K_BLOCK)),
            pl.BlockSpec((BLOCK, KV_WIDTH), lambda b, n: (prev(b, n), Z_K_BLOCK)),
            pl.BlockSpec((BLOCK, KV_WIDTH), lambda b, n: (cur(b, n), Z_V_BLOCK)),
            pl.BlockSpec((BLOCK, KV_WIDTH), lambda b, n: (prev(b, n), Z_V_BLOCK)),
            pl.BlockSpec((CHUNK, SGU_HALF), lambda b, n: (cur(b, n), Z_U_BLOCK)),
            pl.BlockSpec((CHUNK, SGU_HALF), lambda b, n: (cur(b, n), Z_U_BLOCK + 1)),
            pl.BlockSpec((CHUNK, SGU_HALF), lambda b, n: (cur(b, n), Z_VS_BLOCK)),
            pl.BlockSpec((CHUNK, SGU_HALF), lambda b, n: (cur(b, n), Z_VS_BLOCK + 1)),
            pl.BlockSpec((None, N_KV_HEADS, 2 * BLOCK, 4 * BLOCK), lambda b, n: (jnp.minimum(n, 1), 0, 0, 0)),
            pl.BlockSpec(memory_space=pltpu.SMEM),
            whole(1, QGROUP),
            whole(1, QGROUP),
            whole(QGROUP, QGROUP),
            whole(1, SGU_WIDTH),
            whole(1, SGU_WIDTH),
            whole(SGU_GROUPS, CHUNK, CHUNK),
            whole(CHUNK, SGU_GROUPS),
        ],
        out_specs=pl.BlockSpec((BLOCK, MIX_WIDTH), lambda b, n: (cur(b, n), 0)),
        out_shape=jax.ShapeDtypeStruct((TOKENS, MIX_WIDTH), BF16),
        compiler_params=_params(("arbitrary", "arbitrary"), 40 * MIB),
        name="mixer",
    )(z, z, z, z, z, z, z, z, z, band_bias, sinks, q_gain, k_gain, head_ones,
      sgu_ln_w.reshape(1, SGU_WIDTH), sgu_ln_b.reshape(1, SGU_WIDTH), w_spatial, b_spatial.T)


def _mix_out_kernel(y_ref, w_ref, x_ref, mod_ref, o_ref):
    acc = jnp.dot(y_ref[...], w_ref[...], preferred_element_type=F32)
    o_ref[...] = x_ref[...] + mod_ref[2:3, :] * acc


def _mix_out(y, w_out, layer, x, mod):
    tiles_per_seq = SEQ // MIX_OUT_ROWS
    blocks = (2 * MIX_OUT_ROWS * MIX_WIDTH * 2 + 2 * MIX_WIDTH * MIX_OUT_COLS * 2
              + 6 * MIX_OUT_ROWS * MIX_OUT_COLS * 4)
    return pl.pallas_call(
        _mix_out_kernel,
        grid=(TOKENS // MIX_OUT_ROWS, D_MODEL // MIX_OUT_COLS),
        in_specs=[
            pl.BlockSpec((MIX_OUT_ROWS, MIX_WIDTH), lambda i, j: (i, 0)),
            pl.BlockSpec((None, MIX_WIDTH, MIX_OUT_COLS), lambda i, j: (layer, 0, j)),
            pl.BlockSpec((MIX_OUT_ROWS, MIX_OUT_COLS), lambda i, j: (i, j)),
            pl.BlockSpec((None, N_MOD, MIX_OUT_COLS), lambda i, j: (i // tiles_per_seq, 0, j)),
        ],
        out_specs=pl.BlockSpec((MIX_OUT_ROWS, MIX_OUT_COLS), lambda i, j: (i, j)),
        out_shape=jax.ShapeDtypeStruct((TOKENS, D_MODEL), F32),
        compiler_params=_params(("arbitrary", "arbitrary"), blocks + 4 * MIB),
        name="mix_out",
    )(y, w_out, x, mod)


def _ffn_kernel(x_ref, xh_ref, mod_ref, nw_ref, wg_ref, wv_ref, cp_ref, wd_ref, o_ref, h_ref, a0_ref, a1_ref, g0_ref,
                g1_ref):
    i = pl.program_id(0)
    j = pl.program_id(1)
    a_refs, g_refs = (a0_ref, a1_ref), (g0_ref, g1_ref)

    gate_cols, val_cols = slice(0, FF_BLOCK), slice(FF_BLOCK, 2 * FF_BLOCK)

    def up(slot):
        h = h_ref[...]
        a_refs[slot][:, gate_cols] = jnp.dot(h, wg_ref[...], preferred_element_type=F32)
        a_refs[slot][:, val_cols] = jnp.dot(h, wv_ref[...], preferred_element_type=F32)

    def gate(slot, piece):
        a_ref, g_ref = a_refs[slot], g_refs[slot]
        rows = ROW_TILE // FFN_PIECES

        def conv(r0, cols, p0):
            lag = lambda k: a_ref[HALO - k + r0:HALO - k + r0 + GATE_ROWS, cols]
            return (lag(2) * cp_ref[p0:p0 + 1, :] + lag(1) * cp_ref[p0 + 1:p0 + 2, :]
                    + lag(0) * cp_ref[p0 + 2:p0 + 3, :] + cp_ref[p0 + 3:p0 + 4, :])

        for r0 in range(piece * rows, (piece + 1) * rows, GATE_ROWS):
            a_gate = conv(r0, gate_cols, 0)
            a_val = conv(r0, val_cols, CONV_WIDTH + 1)
            g_ref[r0:r0 + GATE_ROWS, :] = (a_gate * jax.nn.sigmoid(a_gate) * a_val).astype(BF16)

    def down(slot, piece):
        cols = slice(piece * (D_MODEL // FFN_PIECES), (piece + 1) * (D_MODEL // FFN_PIECES))
        return cols, jnp.dot(g_refs[slot][...], wd_ref[:, cols], preferred_element_type=F32)

    def gate_and_down(gate_slot, down_slot):
        for piece in range(FFN_PIECES):
            cols, acc = down(down_slot, piece)
            o_ref[:, cols] += acc
            gate(gate_slot, piece)

    last_slot = (N_FF - 1) % 2

    @pl.when(j == 0)
    def _():
        gain, shift = nw_ref[...] * (1.0 + mod_ref[4:5, :]), mod_ref[3:4, :]
        keep = jnp.where(i % (SEQ // ROW_TILE) == 0, 0.0, 1.0)
        h_ref[0:HALO, :] = (_adaln_rows(xh_ref[...], gain, shift) * keep).astype(BF16)
        _norm_into(x_ref, h_ref, HALO, ROW_TILE, gain, shift)
        up(0)
        o_ref[...] = jnp.zeros_like(o_ref)

    @pl.when(j == 1)
    def _():
        up(1)
        for piece in range(FFN_PIECES):
            gate(0, piece)

    for slot in range(2):
        @pl.when((j >= 2) & (j < N_FF) & (lax.rem(j, 2) == slot))
        def _():
            gate_and_down(1 - slot, slot)
            up(slot)

    @pl.when(j == N_FF)
    def _():
        gate_and_down(last_slot, 1 - last_slot)

    @pl.when(j == N_FF + 1)
    def _():
        for piece in range(FFN_PIECES):
            cols, acc = down(last_slot, piece)
            o_ref[:, cols] = x_ref[:, cols] + mod_ref[5:6, cols] * (o_ref[:, cols] + acc)


def _ffn(x, mod, norm_w, w_up, conv_w, conv_b, w_down, layer):
    tiles_per_seq = SEQ // ROW_TILE
    halo_blocks_per_tile = ROW_TILE // HALO
    conv_pack = jnp.concatenate([conv_w, conv_b.reshape(1, 2 * D_FF)], axis=0)
    conv_pack = conv_pack.reshape(CONV_WIDTH + 1, 2, N_FF, FF_BLOCK).transpose(2, 1, 0, 3)
    conv_pack = conv_pack.reshape(N_FF, 2 * (CONV_WIDTH + 1), FF_BLOCK)
    blk = lambda j, lag: jnp.clip(j - lag, 0, N_FF - 1)
    val = lambda b: N_FF + b
    ext_rows = HALO + ROW_TILE
    scratch = ext_rows * D_MODEL * 2 + 2 * ext_rows * 2 * FF_BLOCK * 4 + 2 * ROW_TILE * FF_BLOCK * 2
    blocks = 4 * ROW_TILE * D_MODEL * 4 + 6 * D_MODEL * FF_BLOCK * 2 + scratch + 6 * ext_rows * FF_BLOCK * 4
    return pl.pallas_call(
        _ffn_kernel,
        grid=(TOKENS // ROW_TILE, N_FF + 2),
        in_specs=[
            pl.BlockSpec((ROW_TILE, D_MODEL), lambda i, j: (i, 0)),
            pl.BlockSpec((HALO, D_MODEL), lambda i, j: (jnp.maximum(i * halo_blocks_per_tile - 1, 0), 0)),
            pl.BlockSpec((None, N_MOD, D_MODEL), lambda i, j: (i // tiles_per_seq, 0, 0)),
            pl.BlockSpec((1, D_MODEL), lambda i, j: (0, 0)),
            pl.BlockSpec((None, D_MODEL, FF_BLOCK), lambda i, j: (layer, 0, blk(j, 0))),
            pl.BlockSpec((None, D_MODEL, FF_BLOCK), lambda i, j: (layer, 0, val(blk(j, 0)))),
            pl.BlockSpec((None, 2 * (CONV_WIDTH + 1), FF_BLOCK), lambda i, j: (blk(j, 1), 0, 0)),
            pl.BlockSpec((None, FF_BLOCK, D_MODEL), lambda i, j: (layer, blk(j, 2), 0)),
        ],
        out_specs=pl.BlockSpec((ROW_TILE, D_MODEL), lambda i, j: (i, 0)),
        out_shape=jax.ShapeDtypeStruct((TOKENS, D_MODEL), F32),
        scratch_shapes=[pltpu.VMEM((ext_rows, D_MODEL), BF16)] + 2 * [pltpu.VMEM((ext_rows, 2 * FF_BLOCK), F32)]
        + 2 * [pltpu.VMEM((ROW_TILE, FF_BLOCK), BF16)],
        compiler_params=_params(("arbitrary", "arbitrary"), blocks),
        name="ffn",
    )(x, x, mod, norm_w.reshape(1, D_MODEL), w_up, w_up, conv_pack, w_down)


def kernel(x, c, w_ada, b_ada, ada_table, rel_bias, norm_mix_w, w_in, q_norm_w, k_norm_w, sinks, sgu_ln_w,
           sgu_ln_b, w_spatial, b_spatial, w_out, norm_ffn_w, w_up, conv_w, conv_b, w_down):
    B, S, D = x.shape
    assert (B, S, D) == (BATCH, SEQ, D_MODEL)
    mods = _ada_mod(c, w_ada, b_ada, ada_table).reshape(DEPTH, BATCH, N_MOD, D_MODEL)
    band_bias = _band_bias(rel_bias)
    xt = x.reshape(TOKENS, D_MODEL)
    w_in, w_out, w_up, w_down = (w.astype(BF16) for w in (w_in, w_out, w_up, w_down))
    for l in range(DEPTH):
        mod = mods[l]
        z = _mix_in(xt, mod, norm_mix_w[l], w_in, l)
        y = _mixer(z, band_bias, sinks[l], q_norm_w[l], k_norm_w[l], sgu_ln_w[l], sgu_ln_b[l], w_spatial[l],
                   b_spatial[l])
        xt = _mix_out(y, w_out, l, xt, mod)
        xt = _ffn(xt, mod, norm_ffn_w[l], w_up, conv_w[l], conv_b[l], w_down, l)
    return xt.reshape(B, S, D)
```

```python
import jax
import jax.numpy as jnp
import math
from jax import lax
from jax.experimental import pallas as pl
from jax.experimental.pallas import tpu as pltpu

D_MODEL = 4096
BATCH = 8
SEQ = 2048
DEPTH = 4
HEAD_DIM = 64
N_Q_HEADS = (D_MODEL // 2) // HEAD_DIM
N_KV_HEADS = 8
GQA_GROUP = N_Q_HEADS // N_KV_HEADS
ATTN_WIDTH = N_Q_HEADS * HEAD_DIM
KV_WIDTH = N_KV_HEADS * HEAD_DIM
WINDOW = 128
BLOCK = WINDOW
SGU_WIDTH = D_MODEL - ATTN_WIDTH
SGU_GROUPS = 8
SGU_GROUP_DIM = SGU_WIDTH // SGU_GROUPS
CHUNK = 128
MIX_WIDTH = ATTN_WIDTH + SGU_WIDTH
IN_WIDTH = ATTN_WIDTH + 2 * KV_WIDTH + 2 * SGU_WIDTH
D_FF = 11008
CONV_WIDTH = 3
N_BUCKETS = 32
MAX_DISTANCE = 128
N_MOD = 6
EPS = 1e-6
NEG_INF = -1e30
LOG2E = math.log2(math.e)

TOKENS = BATCH * SEQ
F32 = jnp.float32
BF16 = jnp.bfloat16

V7X_VMEM_BYTES = 64 * 2**20
V7X_LANES = 128
V7X_MXU_DIM = 256
BF16_SUBLANES = 16
MIB = 2**20

ROW_TILE = 512
NORM_CHUNK = 16
NORM_PIECE = 128
MIX_IN_COLS = 1024
MIX_OUT_ROWS = 1024
MIX_OUT_COLS = 512
FF_BLOCK = 256
N_FF = D_FF // FF_BLOCK
HALO = BF16_SUBLANES
GATE_ROWS = 32
FFN_PIECES = 8
ADA_COLS = 512
SGU_HALF = SGU_WIDTH // 2
PAIR = 2 * HEAD_DIM
QGROUP = GQA_GROUP * HEAD_DIM

assert PAIR == V7X_LANES and QGROUP == V7X_MXU_DIM and GQA_GROUP == 4
assert D_FF % FF_BLOCK == 0 and SEQ % ROW_TILE == 0 and SEQ % MIX_OUT_ROWS == 0


def _params(semantics, vmem_bytes):
    return pltpu.CompilerParams(dimension_semantics=semantics,
                                vmem_limit_bytes=int(min(V7X_VMEM_BYTES - 6 * MIB, vmem_bytes)))


def _ada_kernel(c_ref, w_ref, b_ref, tab_ref, o_ref):
    c = c_ref[...]
    a = c * jax.nn.sigmoid(c)
    mod = jnp.dot(a, w_ref[...], preferred_element_type=F32, precision=lax.Precision.HIGHEST) + b_ref[...]
    for l in range(DEPTH):
        o_ref[l] = mod + tab_ref[l:l + 1, :]


def _ada_mod(c, w_ada, b_ada, ada_table):
    width = N_MOD * D_MODEL
    return pl.pallas_call(
        _ada_kernel,
        grid=(width // ADA_COLS,),
        in_specs=[
            pl.BlockSpec((BATCH, D_MODEL), lambda j: (0, 0)),
            pl.BlockSpec((D_MODEL, ADA_COLS), lambda j: (0, j)),
            pl.BlockSpec((1, ADA_COLS), lambda j: (0, j)),
            pl.BlockSpec((DEPTH, ADA_COLS), lambda j: (0, j)),
        ],
        out_specs=pl.BlockSpec((DEPTH, BATCH, ADA_COLS), lambda j: (0, 0, j)),
        out_shape=jax.ShapeDtypeStruct((DEPTH, BATCH, width), F32),
        compiler_params=_params(("arbitrary",), 4 * D_MODEL * ADA_COLS * 4 + 8 * MIB),
        name="ada_mod",
    )(c, w_ada, b_ada.reshape(1, width), ada_table.reshape(DEPTH, width))


def _t5_causal_bucket(dist):
    max_exact = N_BUCKETS // 2
    n = jnp.maximum(dist, 0)
    nf = jnp.maximum(n, 1).astype(jnp.float32)
    large = max_exact + (jnp.log(nf / max_exact) / math.log(MAX_DISTANCE / max_exact)
                         * (N_BUCKETS - max_exact)).astype(jnp.int32)
    large = jnp.minimum(large, N_BUCKETS - 1)
    return jnp.where(n < max_exact, n, large)


def _bias_kernel(rb_ref, bucket_ref, o_ref):
    hk = pl.program_id(1)
    bucket = bucket_ref[...]
    for r in range(GQA_GROUP // 2):
        for par in range(2):
            hq = hk * GQA_GROUP + 2 * r + par
            acc = jnp.full(bucket.shape, NEG_INF, F32)
            for b in range(N_BUCKETS):
                acc = jnp.where(bucket == b, rb_ref[b, hq] * LOG2E, acc)
            o_ref[r * BLOCK:(r + 1) * BLOCK, par * 2 * BLOCK:(par + 1) * 2 * BLOCK] = acc


def _band_bias(rel_bias):
    qi = jnp.arange(BLOCK)[:, None]
    kj = jnp.arange(2 * BLOCK)[None, :]
    dist = qi + BLOCK - kj
    in_window = (dist >= 0) & (dist < WINDOW)
    bucket = jnp.where(in_window, _t5_causal_bucket(dist), -1).astype(jnp.int32)
    buckets = jnp.stack([jnp.where(kj >= BLOCK, bucket, -1), bucket])
    return pl.pallas_call(
        _bias_kernel,
        grid=(2, N_KV_HEADS),
        in_specs=[
            pl.BlockSpec(memory_space=pltpu.SMEM),
            pl.BlockSpec((None, BLOCK, 2 * BLOCK), lambda v, h: (v, 0, 0)),
        ],
        out_specs=pl.BlockSpec((None, None, 2 * BLOCK, 4 * BLOCK), lambda v, h: (v, h, 0, 0)),
        out_shape=jax.ShapeDtypeStruct((2, N_KV_HEADS, 2 * BLOCK, 4 * BLOCK), F32),
        name="band_bias",
    )(rel_bias, buckets)


def _adaln_rows(x, gain, shift):
    ms = jnp.mean(x * x, axis=-1, keepdims=True)
    return x * lax.rsqrt(ms + EPS) * gain + shift


def _norm_rows(x_ref, h_ref, h_row0, row0, rows, gain, shift):
    for r in range(row0, row0 + rows, NORM_CHUNK):
        h_ref[h_row0 + r:h_row0 + r + NORM_CHUNK, :] = _adaln_rows(x_ref[r:r + NORM_CHUNK, :], gain, shift).astype(BF16)


def _mix_in_kernel(x_ref, mod_ref, nw_ref, w_ref, o_ref, h_ref):
    j = pl.program_id(1)

    @pl.when(j == 0)
    def _():
        gain, shift = nw_ref[...] * (1.0 + mod_ref[1:2, :]), mod_ref[0:1, :]
        _norm_rows(x_ref, h_ref, 0, 0, NORM_PIECE, gain, shift)
        for r0 in range(0, ROW_TILE, NORM_PIECE):
            rows = slice(r0, r0 + NORM_PIECE)
            o_ref[rows, :] = jnp.dot(h_ref[rows, :], w_ref[...], preferred_element_type=F32).astype(o_ref.dtype)
            if r0 + NORM_PIECE < ROW_TILE:
                _norm_rows(x_ref, h_ref, 0, r0 + NORM_PIECE, NORM_PIECE, gain, shift)

    @pl.when(j > 0)
    def _():
        o_ref[...] = jnp.dot(h_ref[...], w_ref[...], preferred_element_type=F32).astype(o_ref.dtype)


def _mix_in(x, mod, norm_w, w_in, layer):
    tiles_per_seq = SEQ // ROW_TILE
    blocks = (2 * ROW_TILE * D_MODEL * 4 + ROW_TILE * D_MODEL * 2
              + 2 * D_MODEL * MIX_IN_COLS * 2 + 4 * ROW_TILE * MIX_IN_COLS * 4)
    return pl.pallas_call(
        _mix_in_kernel,
        grid=(TOKENS // ROW_TILE, IN_WIDTH // MIX_IN_COLS),
        in_specs=[
            pl.BlockSpec((ROW_TILE, D_MODEL), lambda i, j: (i, 0)),
            pl.BlockSpec((None, N_MOD, D_MODEL), lambda i, j: (i // tiles_per_seq, 0, 0)),
            pl.BlockSpec((1, D_MODEL), lambda i, j: (0, 0)),
            pl.BlockSpec((None, D_MODEL, MIX_IN_COLS), lambda i, j: (layer, 0, j)),
        ],
        out_specs=pl.BlockSpec((ROW_TILE, MIX_IN_COLS), lambda i, j: (i, j)),
        out_shape=jax.ShapeDtypeStruct((TOKENS, IN_WIDTH), BF16),
        scratch_shapes=[pltpu.VMEM((ROW_TILE, D_MODEL), BF16)],
        compiler_params=_params(("arbitrary", "arbitrary"), blocks + 4 * MIB),
        name="mix_in",
    )(x, mod, norm_w.reshape(1, D_MODEL), w_in)


Z_K_BLOCK = ATTN_WIDTH // KV_WIDTH
Z_V_BLOCK = Z_K_BLOCK + 1
Z_U_BLOCK = (ATTN_WIDTH + 2 * KV_WIDTH) // SGU_HALF
Z_VS_BLOCK = Z_U_BLOCK + 2


def _grouped_rms(t, gain, head_ones):
    rows, width = t.shape
    n = width // QGROUP
    sq = t * t
    stacked = jnp.concatenate([sq[:, c * QGROUP:(c + 1) * QGROUP] for c in range(n)], axis=0).astype(BF16)
    ss = jnp.dot(stacked, head_ones, preferred_element_type=F32)
    rs = lax.rsqrt(ss * (1.0 / HEAD_DIM) + EPS)
    return [t[:, c * QGROUP:(c + 1) * QGROUP] * rs[c * rows:(c + 1) * rows] * gain for c in range(n)]


def _attention(q_ref, kc_ref, kp_ref, vc_ref, vp_ref, bias_ref, sinks_ref, qg_ref, kg_ref, ones_ref, o_ref):
    rows2 = 2 * BLOCK
    head_ones = ones_ref[...]
    lo = lax.broadcasted_iota(jnp.int32, (rows2, PAIR), 1) < HEAD_DIM
    top = lax.broadcasted_iota(jnp.int32, (rows2, 1), 0) < BLOCK
    ones_cols = jnp.concatenate([jnp.where(lo, 1.0, 0.0), jnp.where(lo, 0.0, 1.0)], axis=0).astype(BF16)

    qn = _grouped_rms(q_ref[...].astype(F32), qg_ref[...], head_ones)
    kn = _grouped_rms(jnp.concatenate([kp_ref[...], kc_ref[...]], axis=0).astype(F32), kg_ref[...], head_ones)
    vf = jnp.concatenate([vp_ref[...], vc_ref[...]], axis=0).astype(F32)

    def split_pair(pair):
        swapped = pltpu.roll(pair, HEAD_DIM, axis=1)
        a = jnp.concatenate([jnp.where(lo, pair, 0.0), jnp.where(lo, 0.0, swapped)], axis=0)
        b = jnp.concatenate([jnp.where(lo, swapped, 0.0), jnp.where(lo, 0.0, pair)], axis=0)
        return a.astype(BF16), b.astype(BF16)

    kk, vv = [], []
    for p in range(N_KV_HEADS // 2):
        kk.extend(split_pair(kn[p // 2][:, (p % 2) * PAIR:(p % 2 + 1) * PAIR]))
        vv.extend(split_pair(vf[:, p * PAIR:(p + 1) * PAIR]))

    def scores(hk):
        q2 = jnp.concatenate([qn[hk][:, :PAIR], qn[hk][:, PAIR:]], axis=0).astype(BF16)
        s = lax.dot_general(q2, kk[hk], (((1,), (1,)), ((), ())), preferred_element_type=F32)
        return s + bias_ref[hk]

    def softmax_pv(hk, s):
        probs, sink_terms = [], []
        for par in range(2):
            sp = s[:, par * rows2:(par + 1) * rows2]
            sink = jnp.where(top, sinks_ref[hk * GQA_GROUP + par], sinks_ref[hk * GQA_GROUP + 2 + par])
            m = jnp.maximum(jnp.max(sp, axis=-1, keepdims=True), sink)
            probs.append(jnp.exp2(sp - m).astype(BF16))
            sink_terms.append(jnp.exp2(sink - m))
        nd = jnp.dot(jnp.concatenate(probs, axis=1), jnp.concatenate([vv[hk], ones_cols], axis=1),
                     preferred_element_type=F32)
        den = nd[:, PAIR:] + jnp.where(lo, sink_terms[0], sink_terms[1])
        out = (nd[:, :PAIR] / den).astype(o_ref.dtype)
        o_ref[:, (2 * hk) * PAIR:(2 * hk + 1) * PAIR] = out[:BLOCK]
        o_ref[:, (2 * hk + 1) * PAIR:(2 * hk + 2) * PAIR] = out[BLOCK:]

    s_next = scores(0)
    for hk in range(N_KV_HEADS):
        s = s_next
        if hk + 1 < N_KV_HEADS:
            s_next = scores(hk + 1)
        softmax_pv(hk, s)


def _gelu(t):
    return 0.5 * t * (1.0 + lax.erf(t * (2.0 ** -0.5)))


def _sgu(u_refs, vs_refs, lnw_ref, lnb_ref, ws_ref, bst_ref, o_ref):
    v = _gelu(jnp.concatenate([r[...] for r in vs_refs], axis=1).astype(F32))
    mu = jnp.mean(v, axis=-1, keepdims=True)
    vc = v - mu
    var = jnp.mean(vc * vc, axis=-1, keepdims=True)
    vn = (vc * lax.rsqrt(var + EPS) * lnw_ref[...] + lnb_ref[...]).astype(BF16)
    row = lax.broadcasted_iota(jnp.int32, (CHUNK, CHUNK), 0)
    col = lax.broadcasted_iota(jnp.int32, (CHUNK, CHUNK), 1)
    causal = col <= row
    groups_per_ref = SGU_HALF // SGU_GROUP_DIM
    for g in range(SGU_GROUPS):
        ws = jnp.where(causal, ws_ref[g], 0.0).astype(BF16)
        mixed = (jnp.dot(ws, vn[:, g * SGU_GROUP_DIM:(g + 1) * SGU_GROUP_DIM], preferred_element_type=F32)
                 + bst_ref[:, g:g + 1])
        gr = g % groups_per_ref
        u = _gelu(u_refs[g // groups_per_ref][:, gr * SGU_GROUP_DIM:(gr + 1) * SGU_GROUP_DIM].astype(F32))
        o_ref[:, ATTN_WIDTH + g * SGU_GROUP_DIM:ATTN_WIDTH + (g + 1) * SGU_GROUP_DIM] = (
            u * mixed).astype(o_ref.dtype)


def _mixer_kernel(q_ref, kc_ref, kp_ref, vc_ref, vp_ref, u0_ref, u1_ref, vs0_ref, vs1_ref, bias_ref, sinks_ref,
                  qg_ref, kg_ref, ones_ref, lnw_ref, lnb_ref, ws_ref, bst_ref, o_ref):
    _attention(q_ref, kc_ref, kp_ref, vc_ref, vp_ref, bias_ref, sinks_ref, qg_ref, kg_ref, ones_ref, o_ref)
    _sgu((u0_ref, u1_ref), (vs0_ref, vs1_ref), lnw_ref, lnb_ref, ws_ref, bst_ref, o_ref)


def _mixer(z, band_bias, sinks, q_norm_w, k_norm_w, sgu_ln_w, sgu_ln_b, w_spatial, b_spatial):
    nb = SEQ // BLOCK
    cur = lambda b, n: b * nb + n
    prev = lambda b, n: b * nb + jnp.maximum(n - 1, 0)
    whole = lambda *shape: pl.BlockSpec(shape, lambda b, n: (0,) * len(shape))
    q_gain = jnp.tile(q_norm_w * (HEAD_DIM ** -0.5 * LOG2E), GQA_GROUP).reshape(1, QGROUP)
    k_gain = jnp.tile(k_norm_w, GQA_GROUP).reshape(1, QGROUP)
    head_ones = jnp.kron(jnp.eye(GQA_GROUP, dtype=F32), jnp.ones((HEAD_DIM, HEAD_DIM), F32)).astype(BF16)
    return pl.pallas_call(
        _mixer_kernel,
        grid=(BATCH, nb),
        in_specs=[
            pl.BlockSpec((BLOCK, ATTN_WIDTH), lambda b, n: (cur(b, n), 0)),
            pl.BlockSpec((BLOCK, KV_WIDTH), lambda b, n: (cur(b, n), Z_K_BLOCK)),
            pl.BlockSpec((BLOCK, KV_WIDTH), lambda b, n: (prev(b, n), Z_K_BLOCK)),
            pl.BlockSpec((BLOCK, KV_WIDTH), lambda b, n: (cur(b, n), Z_V_BLOCK)),
            pl.BlockSpec((BLOCK, KV_WIDTH), lambda b, n: (prev(b, n), Z_V_BLOCK)),
            pl.BlockSpec((CHUNK, SGU_HALF), lambda b, n: (cur(b, n), Z_U_BLOCK)),
            pl.BlockSpec((CHUNK, SGU_HALF), lambda b, n: (cur(b, n), Z_U_BLOCK + 1)),
            pl.BlockSpec((CHUNK, SGU_HALF), lambda b, n: (cur(b, n), Z_VS_BLOCK)),
            pl.BlockSpec((CHUNK, SGU_HALF), lambda b, n: (cur(b, n), Z_VS_BLOCK + 1)),
            pl.BlockSpec((None, N_KV_HEADS, 2 * BLOCK, 4 * BLOCK), lambda b, n: (jnp.minimum(n, 1), 0, 0, 0)),
            pl.BlockSpec(memory_space=pltpu.SMEM),
            whole(1, QGROUP),
            whole(1, QGROUP),
            whole(QGROUP, QGROUP),
            whole(1, SGU_WIDTH),
            whole(1, SGU_WIDTH),
            whole(SGU_GROUPS, CHUNK, CHUNK),
            whole(CHUNK, SGU_GROUPS),
        ],
        out_specs=pl.BlockSpec((BLOCK, MIX_WIDTH), lambda b, n: (cur(b, n), 0)),
        out_shape=jax.ShapeDtypeStruct((TOKENS, MIX_WIDTH), BF16),
        compiler_params=_params(("arbitrary", "arbitrary"), 40 * MIB),
        name="mixer",
    )(z, z, z, z, z, z, z, z, z, band_bias, sinks * LOG2E, q_gain, k_gain, head_ones,
      sgu_ln_w.reshape(1, SGU_WIDTH), sgu_ln_b.reshape(1, SGU_WIDTH), w_spatial, b_spatial.T)


def _mix_out_kernel(y_ref, w_ref, x_ref, mod_ref, o_ref):
    acc = jnp.dot(y_ref[...], w_ref[...], preferred_element_type=F32)
    o_ref[...] = x_ref[...] + mod_ref[2:3, :] * acc


def _mix_out(y, w_out, layer, x, mod):
    tiles_per_seq = SEQ // MIX_OUT_ROWS
    blocks = (2 * MIX_OUT_ROWS * MIX_WIDTH * 2 + 2 * MIX_WIDTH * MIX_OUT_COLS * 2
              + 6 * MIX_OUT_ROWS * MIX_OUT_COLS * 4)
    return pl.pallas_call(
        _mix_out_kernel,
        grid=(TOKENS // MIX_OUT_ROWS, D_MODEL // MIX_OUT_COLS),
        in_specs=[
            pl.BlockSpec((MIX_OUT_ROWS, MIX_WIDTH), lambda i, j: (i, 0)),
            pl.BlockSpec((None, MIX_WIDTH, MIX_OUT_COLS), lambda i, j: (layer, 0, j)),
            pl.BlockSpec((MIX_OUT_ROWS, MIX_OUT_COLS), lambda i, j: (i, j)),
            pl.BlockSpec((None, N_MOD, MIX_OUT_COLS), lambda i, j: (i // tiles_per_seq, 0, j)),
        ],
        out_specs=pl.BlockSpec((MIX_OUT_ROWS, MIX_OUT_COLS), lambda i, j: (i, j)),
        out_shape=jax.ShapeDtypeStruct((TOKENS, D_MODEL), F32),
        compiler_params=_params(("arbitrary", "arbitrary"), blocks + 4 * MIB),
        name="mix_out",
    )(y, w_out, x, mod)


def _ffn_kernel(x_ref, xh_ref, mod_ref, nw_ref, wg_ref, wv_ref, cp_ref, wd_ref, o_ref, h_ref, a0_ref, a1_ref, g0_ref,
                g1_ref):
    i = pl.program_id(0)
    j = pl.program_id(1)
    a_refs, g_refs = (a0_ref, a1_ref), (g0_ref, g1_ref)

    gate_cols, val_cols = slice(0, FF_BLOCK), slice(FF_BLOCK, 2 * FF_BLOCK)

    def up(slot, rows=slice(None)):
        h = h_ref[rows, :]
        a_refs[slot][rows, gate_cols] = jnp.dot(h, wg_ref[...], preferred_element_type=F32)
        a_refs[slot][rows, val_cols] = jnp.dot(h, wv_ref[...], preferred_element_type=F32)

    def gate(slot, piece):
        a_ref, g_ref = a_refs[slot], g_refs[slot]
        rows = ROW_TILE // FFN_PIECES

        def conv(r0, cols, p0):
            lag = lambda k: a_ref[HALO - k + r0:HALO - k + r0 + GATE_ROWS, cols]
            return (lag(2) * cp_ref[p0:p0 + 1, :] + lag(1) * cp_ref[p0 + 1:p0 + 2, :]
                    + lag(0) * cp_ref[p0 + 2:p0 + 3, :] + cp_ref[p0 + 3:p0 + 4, :])

        for r0 in range(piece * rows, (piece + 1) * rows, GATE_ROWS):
            a_gate = conv(r0, gate_cols, 0)
            a_val = conv(r0, val_cols, CONV_WIDTH + 1)
            g_ref[r0:r0 + GATE_ROWS, :] = (a_gate * jax.nn.sigmoid(a_gate) * a_val).astype(BF16)

    def down(slot, piece):
        cols = slice(piece * (D_MODEL // FFN_PIECES), (piece + 1) * (D_MODEL // FFN_PIECES))
        return cols, jnp.dot(g_refs[slot][...], wd_ref[:, cols], preferred_element_type=F32)

    def gate_and_down(gate_slot, down_slot):
        for piece in range(FFN_PIECES):
            cols, acc = down(down_slot, piece)
            o_ref[:, cols] += acc
            gate(gate_slot, piece)

    last_slot = (N_FF - 1) % 2

    @pl.when(j == 0)
    def _():
        gain, shift = nw_ref[...] * (1.0 + mod_ref[4:5, :]), mod_ref[3:4, :]
        keep = jnp.where(i % (SEQ // ROW_TILE) == 0, 0.0, 1.0)
        h_ref[0:HALO, :] = (_adaln_rows(xh_ref[...], gain, shift) * keep).astype(BF16)
        o_ref[...] = jnp.zeros_like(o_ref)
        _norm_rows(x_ref, h_ref, HALO, 0, NORM_PIECE, gain, shift)
        for r0 in range(0, ROW_TILE, NORM_PIECE):
            up(0, slice(HALO + r0 if r0 else 0, HALO + r0 + NORM_PIECE))
            if r0 + NORM_PIECE < ROW_TILE:
                _norm_rows(x_ref, h_ref, HALO, r0 + NORM_PIECE, NORM_PIECE, gain, shift)

    @pl.when(j == 1)
    def _():
        up(1)
        for piece in range(FFN_PIECES):
            gate(0, piece)

    for slot in range(2):
        @pl.when((j >= 2) & (j < N_FF) & (lax.rem(j, 2) == slot))
        def _():
            gate_and_down(1 - slot, slot)
            up(slot)

    @pl.when(j == N_FF)
    def _():
        gate_and_down(last_slot, 1 - last_slot)

    @pl.when(j == N_FF + 1)
    def _():
        for piece in range(FFN_PIECES):
            cols, acc = down(last_slot, piece)
            o_ref[:, cols] = x_ref[:, cols] + mod_ref[5:6, cols] * (o_ref[:, cols] + acc)


def _ffn(x, mod, norm_w, w_up, conv_w, conv_b, w_down, layer):
    tiles_per_seq = SEQ // ROW_TILE
    halo_blocks_per_tile = ROW_TILE // HALO
    conv_pack = jnp.concatenate([conv_w, conv_b.reshape(1, 2 * D_FF)], axis=0)
    conv_pack = conv_pack.reshape(CONV_WIDTH + 1, 2, N_FF, FF_BLOCK).transpose(2, 1, 0, 3)
    conv_pack = conv_pack.reshape(N_FF, 2 * (CONV_WIDTH + 1), FF_BLOCK)
    blk = lambda j, lag: jnp.clip(j - lag, 0, N_FF - 1)
    val = lambda b: N_FF + b
    ext_rows = HALO + ROW_TILE
    scratch = ext_rows * D_MODEL * 2 + 2 * ext_rows * 2 * FF_BLOCK * 4 + 2 * ROW_TILE * FF_BLOCK * 2
    blocks = 4 * ROW_TILE * D_MODEL * 4 + 6 * D_MODEL * FF_BLOCK * 2 + scratch + 6 * ext_rows * FF_BLOCK * 4
    return pl.pallas_call(
        _ffn_kernel,
        grid=(TOKENS // ROW_TILE, N_FF + 2),
        in_specs=[
            pl.BlockSpec((ROW_TILE, D_MODEL), lambda i, j: (i, 0)),
            pl.BlockSpec((HALO, D_MODEL), lambda i, j: (jnp.maximum(i * halo_blocks_per_tile - 1, 0), 0)),
            pl.BlockSpec((None, N_MOD, D_MODEL), lambda i, j: (i // tiles_per_seq, 0, 0)),
            pl.BlockSpec((1, D_MODEL), lambda i, j: (0, 0)),
            pl.BlockSpec((None, D_MODEL, FF_BLOCK), lambda i, j: (layer, 0, blk(j, 0))),
            pl.BlockSpec((None, D_MODEL, FF_BLOCK), lambda i, j: (layer, 0, val(blk(j, 0)))),
            pl.BlockSpec((None, 2 * (CONV_WIDTH + 1), FF_BLOCK), lambda i, j: (blk(j, 1), 0, 0)),
            pl.BlockSpec((None, FF_BLOCK, D_MODEL), lambda i, j: (layer, blk(j, 2), 0)),
        ],
        out_specs=pl.BlockSpec((ROW_TILE, D_MODEL), lambda i, j: (i, 0)),
        out_shape=jax.ShapeDtypeStruct((TOKENS, D_MODEL), F32),
        scratch_shapes=[pltpu.VMEM((ext_rows, D_MODEL), BF16)] + 2 * [pltpu.VMEM((ext_rows, 2 * FF_BLOCK), F32)]
        + 2 * [pltpu.VMEM((ROW_TILE, FF_BLOCK), BF16)],
        compiler_params=_params(("arbitrary", "arbitrary"), blocks),
        name="ffn",
    )(x, x, mod, norm_w.reshape(1, D_MODEL), w_up, w_up, conv_pack, w_down)


def kernel(x, c, w_ada, b_ada, ada_table, rel_bias, norm_mix_w, w_in, q_norm_w, k_norm_w, sinks, sgu_ln_w,
           sgu_ln_b, w_spatial, b_spatial, w_out, norm_ffn_w, w_up, conv_w, conv_b, w_down):
    B, S, D = x.shape
    assert (B, S, D) == (BATCH, SEQ, D_MODEL)
    mods = _ada_mod(c, w_ada, b_ada, ada_table).reshape(DEPTH, BATCH, N_MOD, D_MODEL)
    band_bias = _band_bias(rel_bias)
    xt = x.reshape(TOKENS, D_MODEL)
    w_in, w_out, w_up, w_down = (w.astype(BF16) for w in (w_in, w_out, w_up, w_down))
    for l in range(DEPTH):
        mod = mods[l]
        z = _mix_in(xt, mod, norm_mix_w[l], w_in, l)
        y = _mixer(z, band_bias, sinks[l], q_norm_w[l], k_norm_w[l], sgu_ln_w[l], sgu_ln_b[l], w_spatial[l],
                   b_spatial[l])
        xt = _mix_out(y, w_out, l, xt, mod)
        xt = _ffn(xt, mod, norm_ffn_w[l], w_up, conv_w[l], conv_b[l], w_down, l)
    return xt.reshape(B, S, D)
```

```python
import jax
import jax.numpy as jnp
import math
from jax import lax
from jax.experimental import pallas as pl
from jax.experimental.pallas import tpu as pltpu

D_MODEL = 4096
BATCH = 8
SEQ = 2048
DEPTH = 4
HEAD_DIM = 64
N_Q_HEADS = (D_MODEL // 2) // HEAD_DIM
N_KV_HEADS = 8
GQA_GROUP = N_Q_HEADS // N_KV_HEADS
ATTN_WIDTH = N_Q_HEADS * HEAD_DIM
KV_WIDTH = N_KV_HEADS * HEAD_DIM
WINDOW = 128
BLOCK = WINDOW
SGU_WIDTH = D_MODEL - ATTN_WIDTH
SGU_GROUPS = 8
SGU_GROUP_DIM = SGU_WIDTH // SGU_GROUPS
CHUNK = 128
MIX_WIDTH = ATTN_WIDTH + SGU_WIDTH
IN_WIDTH = ATTN_WIDTH + 2 * KV_WIDTH + 2 * SGU_WIDTH
D_FF = 11008
CONV_WIDTH = 3
N_BUCKETS = 32
MAX_DISTANCE = 128
N_MOD = 6
EPS = 1e-6
NEG_INF = -1e30
LOG2E = math.log2(math.e)

TOKENS = BATCH * SEQ
F32 = jnp.float32
BF16 = jnp.bfloat16

V7X_VMEM_BYTES = 64 * 2**20
V7X_LANES = 128
V7X_MXU_DIM = 256
BF16_SUBLANES = 16
MIB = 2**20

ROW_TILE = 512
NORM_CHUNK = 16
NORM_PIECE = 128
MIX_IN_COLS = 1024
MIX_OUT_ROWS = 1024
MIX_OUT_COLS = 512
FF_BLOCK = 256
N_FF = D_FF // FF_BLOCK
HALO = BF16_SUBLANES
GATE_ROWS = 32
FFN_PIECES = 8
ADA_COLS = 512
SGU_HALF = SGU_WIDTH // 2
PAIR = 2 * HEAD_DIM
QGROUP = GQA_GROUP * HEAD_DIM

assert PAIR == V7X_LANES and QGROUP == V7X_MXU_DIM and GQA_GROUP == 4
assert D_FF % FF_BLOCK == 0 and SEQ % ROW_TILE == 0 and SEQ % MIX_OUT_ROWS == 0


def _params(semantics, vmem_bytes):
    return pltpu.CompilerParams(dimension_semantics=semantics,
                                vmem_limit_bytes=int(min(V7X_VMEM_BYTES - 6 * MIB, vmem_bytes)))


def _ada_kernel(c_ref, w_ref, b_ref, tab_ref, o_ref):
    c = c_ref[...]
    a = c * jax.nn.sigmoid(c)
    mod = jnp.dot(a, w_ref[...], preferred_element_type=F32, precision=lax.Precision.HIGHEST) + b_ref[...]
    for l in range(DEPTH):
        o_ref[l] = mod + tab_ref[l:l + 1, :]


def _ada_mod(c, w_ada, b_ada, ada_table):
    width = N_MOD * D_MODEL
    return pl.pallas_call(
        _ada_kernel,
        grid=(width // ADA_COLS,),
        in_specs=[
            pl.BlockSpec((BATCH, D_MODEL), lambda j: (0, 0)),
            pl.BlockSpec((D_MODEL, ADA_COLS), lambda j: (0, j)),
            pl.BlockSpec((1, ADA_COLS), lambda j: (0, j)),
            pl.BlockSpec((DEPTH, ADA_COLS), lambda j: (0, j)),
        ],
        out_specs=pl.BlockSpec((DEPTH, BATCH, ADA_COLS), lambda j: (0, 0, j)),
        out_shape=jax.ShapeDtypeStruct((DEPTH, BATCH, width), F32),
        compiler_params=_params(("arbitrary",), 4 * D_MODEL * ADA_COLS * 4 + 8 * MIB),
        name="ada_mod",
    )(c, w_ada, b_ada.reshape(1, width), ada_table.reshape(DEPTH, width))


def _t5_causal_bucket(dist):
    max_exact = N_BUCKETS // 2
    n = jnp.maximum(dist, 0)
    nf = jnp.maximum(n, 1).astype(jnp.float32)
    large = max_exact + (jnp.log(nf / max_exact) / math.log(MAX_DISTANCE / max_exact)
                         * (N_BUCKETS - max_exact)).astype(jnp.int32)
    large = jnp.minimum(large, N_BUCKETS - 1)
    return jnp.where(n < max_exact, n, large)


def _bias_kernel(rb_ref, bucket_ref, o_ref):
    hk = pl.program_id(1)
    bucket = bucket_ref[...]
    for r in range(GQA_GROUP // 2):
        for par in range(2):
            hq = hk * GQA_GROUP + 2 * r + par
            acc = jnp.full(bucket.shape, NEG_INF, F32)
            for b in range(N_BUCKETS):
                acc = jnp.where(bucket == b, rb_ref[b, hq] * LOG2E, acc)
            o_ref[r * BLOCK:(r + 1) * BLOCK, par * 2 * BLOCK:(par + 1) * 2 * BLOCK] = acc


def _band_bias(rel_bias):
    qi = jnp.arange(BLOCK)[:, None]
    kj = jnp.arange(2 * BLOCK)[None, :]
    dist = qi + BLOCK - kj
    in_window = (dist >= 0) & (dist < WINDOW)
    bucket = jnp.where(in_window, _t5_causal_bucket(dist), -1).astype(jnp.int32)
    buckets = jnp.stack([jnp.where(kj >= BLOCK, bucket, -1), bucket])
    return pl.pallas_call(
        _bias_kernel,
        grid=(2, N_KV_HEADS),
        in_specs=[
            pl.BlockSpec(memory_space=pltpu.SMEM),
            pl.BlockSpec((None, BLOCK, 2 * BLOCK), lambda v, h: (v, 0, 0)),
        ],
        out_specs=pl.BlockSpec((None, None, 2 * BLOCK, 4 * BLOCK), lambda v, h: (v, h, 0, 0)),
        out_shape=jax.ShapeDtypeStruct((2, N_KV_HEADS, 2 * BLOCK, 4 * BLOCK), F32),
        name="band_bias",
    )(rel_bias, buckets)


def _adaln_rows(x, gain, shift):
    ms = jnp.mean(x * x, axis=-1, keepdims=True)
    return x * lax.rsqrt(ms + EPS) * gain + shift


def _norm_rows(x_ref, h_ref, h_row0, row0, rows, gain, shift):
    for r in range(row0, row0 + rows, NORM_CHUNK):
        h_ref[h_row0 + r:h_row0 + r + NORM_CHUNK, :] = _adaln_rows(x_ref[r:r + NORM_CHUNK, :], gain, shift).astype(BF16)


def _mix_in_kernel(x_ref, mod_ref, nw_ref, w_ref, o_ref, h_ref):
    j = pl.program_id(1)

    @pl.when(j == 0)
    def _():
        gain, shift = nw_ref[...] * (1.0 + mod_ref[1:2, :]), mod_ref[0:1, :]
        _norm_rows(x_ref, h_ref, 0, 0, NORM_PIECE, gain, shift)
        for r0 in range(0, ROW_TILE, NORM_PIECE):
            rows = slice(r0, r0 + NORM_PIECE)
            o_ref[rows, :] = jnp.dot(h_ref[rows, :], w_ref[...], preferred_element_type=F32).astype(o_ref.dtype)
            if r0 + NORM_PIECE < ROW_TILE:
                _norm_rows(x_ref, h_ref, 0, r0 + NORM_PIECE, NORM_PIECE, gain, shift)

    @pl.when(j > 0)
    def _():
        o_ref[...] = jnp.dot(h_ref[...], w_ref[...], preferred_element_type=F32).astype(o_ref.dtype)


def _mix_in(x, mod, norm_w, w_in):
    tiles_per_seq = SEQ // ROW_TILE
    blocks = (2 * ROW_TILE * D_MODEL * 4 + ROW_TILE * D_MODEL * 2
              + 2 * D_MODEL * MIX_IN_COLS * 2 + 4 * ROW_TILE * MIX_IN_COLS * 4)
    return pl.pallas_call(
        _mix_in_kernel,
        grid=(TOKENS // ROW_TILE, IN_WIDTH // MIX_IN_COLS),
        in_specs=[
            pl.BlockSpec((ROW_TILE, D_MODEL), lambda i, j: (i, 0)),
            pl.BlockSpec((None, N_MOD, D_MODEL), lambda i, j: (i // tiles_per_seq, 0, 0)),
            pl.BlockSpec((1, D_MODEL), lambda i, j: (0, 0)),
            pl.BlockSpec((D_MODEL, MIX_IN_COLS), lambda i, j: (0, j)),
        ],
        out_specs=pl.BlockSpec((ROW_TILE, MIX_IN_COLS), lambda i, j: (i, j)),
        out_shape=jax.ShapeDtypeStruct((TOKENS, IN_WIDTH), BF16),
        scratch_shapes=[pltpu.VMEM((ROW_TILE, D_MODEL), BF16)],
        compiler_params=_params(("arbitrary", "arbitrary"), blocks + 4 * MIB),
        name="mix_in",
    )(x, mod, norm_w.reshape(1, D_MODEL), w_in)


Z_K_BLOCK = ATTN_WIDTH // KV_WIDTH
Z_V_BLOCK = Z_K_BLOCK + 1
Z_U_BLOCK = (ATTN_WIDTH + 2 * KV_WIDTH) // SGU_HALF
Z_VS_BLOCK = Z_U_BLOCK + 2


def _grouped_rms(t, gain, head_ones):
    rows, width = t.shape
    n = width // QGROUP
    sq = t * t
    stacked = jnp.concatenate([sq[:, c * QGROUP:(c + 1) * QGROUP] for c in range(n)], axis=0).astype(BF16)
    ss = jnp.dot(stacked, head_ones, preferred_element_type=F32)
    rs = lax.rsqrt(ss * (1.0 / HEAD_DIM) + EPS)
    return [t[:, c * QGROUP:(c + 1) * QGROUP] * rs[c * rows:(c + 1) * rows] * gain for c in range(n)]


def _attention(q_ref, kc_ref, kp_ref, vc_ref, vp_ref, bias_ref, sinks_ref, qg_ref, kg_ref, ones_ref, o_ref):
    rows2 = 2 * BLOCK
    head_ones = ones_ref[...]
    lo = lax.broadcasted_iota(jnp.int32, (rows2, PAIR), 1) < HEAD_DIM
    top = lax.broadcasted_iota(jnp.int32, (rows2, 1), 0) < BLOCK
    ones_cols = jnp.concatenate([jnp.where(lo, 1.0, 0.0), jnp.where(lo, 0.0, 1.0)], axis=0).astype(BF16)

    qn = _grouped_rms(q_ref[...].astype(F32), qg_ref[...], head_ones)
    kn = _grouped_rms(jnp.concatenate([kp_ref[...], kc_ref[...]], axis=0).astype(F32), kg_ref[...], head_ones)
    vf = jnp.concatenate([vp_ref[...], vc_ref[...]], axis=0).astype(F32)

    def split_pair(pair):
        swapped = pltpu.roll(pair, HEAD_DIM, axis=1)
        a = jnp.concatenate([jnp.where(lo, pair, 0.0), jnp.where(lo, 0.0, swapped)], axis=0)
        b = jnp.concatenate([jnp.where(lo, swapped, 0.0), jnp.where(lo, 0.0, pair)], axis=0)
        return a.astype(BF16), b.astype(BF16)

    kk, vv = [], []
    for p in range(N_KV_HEADS // 2):
        kk.extend(split_pair(kn[p // 2][:, (p % 2) * PAIR:(p % 2 + 1) * PAIR]))
        vv.extend(split_pair(vf[:, p * PAIR:(p + 1) * PAIR]))

    def scores(hk):
        q2 = jnp.concatenate([qn[hk][:, :PAIR], qn[hk][:, PAIR:]], axis=0).astype(BF16)
        s = lax.dot_general(q2, kk[hk], (((1,), (1,)), ((), ())), preferred_element_type=F32)
        return s + bias_ref[hk]

    def softmax_pv(hk, s):
        probs, sink_terms = [], []
        for par in range(2):
            sp = s[:, par * rows2:(par + 1) * rows2]
            sink = jnp.where(top, sinks_ref[hk * GQA_GROUP + par], sinks_ref[hk * GQA_GROUP + 2 + par])
            m = jnp.maximum(jnp.max(sp, axis=-1, keepdims=True), sink)
            probs.append(jnp.exp2(sp - m).astype(BF16))
            sink_terms.append(jnp.exp2(sink - m))
        nd = jnp.dot(jnp.concatenate(probs, axis=1), jnp.concatenate([vv[hk], ones_cols], axis=1),
                     preferred_element_type=F32)
        den = nd[:, PAIR:] + jnp.where(lo, sink_terms[0], sink_terms[1])
        out = (nd[:, :PAIR] / den).astype(o_ref.dtype)
        o_ref[:, (2 * hk) * PAIR:(2 * hk + 1) * PAIR] = out[:BLOCK]
        o_ref[:, (2 * hk + 1) * PAIR:(2 * hk + 2) * PAIR] = out[BLOCK:]

    s_next = scores(0)
    for hk in range(N_KV_HEADS):
        s = s_next
        if hk + 1 < N_KV_HEADS:
            s_next = scores(hk + 1)
        softmax_pv(hk, s)


def _gelu(t):
    return 0.5 * t * (1.0 + lax.erf(t * (2.0 ** -0.5)))


def _sgu(u_refs, vs_refs, lnw_ref, lnb_ref, ws_ref, bst_ref, o_ref):
    v = _gelu(jnp.concatenate([r[...] for r in vs_refs], axis=1).astype(F32))
    mu = jnp.mean(v, axis=-1, keepdims=True)
    vc = v - mu
    var = jnp.mean(vc * vc, axis=-1, keepdims=True)
    vn = (vc * lax.rsqrt(var + EPS) * lnw_ref[...] + lnb_ref[...]).astype(BF16)
    row = lax.broadcasted_iota(jnp.int32, (CHUNK, CHUNK), 0)
    col = lax.broadcasted_iota(jnp.int32, (CHUNK, CHUNK), 1)
    causal = col <= row
    groups_per_ref = SGU_HALF // SGU_GROUP_DIM
    for g in range(SGU_GROUPS):
        ws = jnp.where(causal, ws_ref[g], 0.0).astype(BF16)
        mixed = (jnp.dot(ws, vn[:, g * SGU_GROUP_DIM:(g + 1) * SGU_GROUP_DIM], preferred_element_type=F32)
                 + bst_ref[:, g:g + 1])
        gr = g % groups_per_ref
        u = _gelu(u_refs[g // groups_per_ref][:, gr * SGU_GROUP_DIM:(gr + 1) * SGU_GROUP_DIM].astype(F32))
        o_ref[:, ATTN_WIDTH + g * SGU_GROUP_DIM:ATTN_WIDTH + (g + 1) * SGU_GROUP_DIM] = (
            u * mixed).astype(o_ref.dtype)


def _mixer_kernel(q_ref, kc_ref, kp_ref, vc_ref, vp_ref, u0_ref, u1_ref, vs0_ref, vs1_ref, bias_ref, sinks_ref,
                  qg_ref, kg_ref, ones_ref, lnw_ref, lnb_ref, ws_ref, bst_ref, *rest):
    o_ref = rest[len(rest) // 2] if len(rest) > 1 else rest[0]
    _attention(q_ref, kc_ref, kp_ref, vc_ref, vp_ref, bias_ref, sinks_ref, qg_ref, kg_ref, ones_ref, o_ref)
    _sgu((u0_ref, u1_ref), (vs0_ref, vs1_ref), lnw_ref, lnb_ref, ws_ref, bst_ref, o_ref)
    n_cast = len(rest) // 2
    for src_ref, dst_ref in zip(rest[:n_cast], rest[n_cast + 1:]):
        dst_ref[...] = src_ref[...].astype(dst_ref.dtype)


def _mixer(z, band_bias, sinks, q_norm_w, k_norm_w, sgu_ln_w, sgu_ln_b, w_spatial, b_spatial, cast=None):
    nb = SEQ // BLOCK
    n_steps = BATCH * nb
    cast_in, cast_out, cast_shapes, cast_args, cast_bytes = [], [], [], [], 0
    if cast is not None:
        weights, layer = cast
        for w in weights:
            _, rows, cols = w.shape
            chunk = next(r for r in range(BF16_SUBLANES, rows + 1, BF16_SUBLANES)
                         if rows % r == 0 and r * n_steps >= rows)
            last = rows // chunk - 1
            cast_in.append(pl.BlockSpec((None, chunk, cols),
                                        lambda b, n, last=last: (layer, jnp.minimum(b * nb + n, last), 0)))
            cast_out.append(pl.BlockSpec((chunk, cols), lambda b, n, last=last: (jnp.minimum(b * nb + n, last), 0)))
            cast_shapes.append(jax.ShapeDtypeStruct((rows, cols), BF16))
            cast_args.append(w)
            cast_bytes += 2 * chunk * cols * (4 + 2)
    cur = lambda b, n: b * nb + n
    prev = lambda b, n: b * nb + jnp.maximum(n - 1, 0)
    whole = lambda *shape: pl.BlockSpec(shape, lambda b, n: (0,) * len(shape))
    q_gain = jnp.tile(q_norm_w * (HEAD_DIM ** -0.5 * LOG2E), GQA_GROUP).reshape(1, QGROUP)
    k_gain = jnp.tile(k_norm_w, GQA_GROUP).reshape(1, QGROUP)
    head_ones = jnp.kron(jnp.eye(GQA_GROUP, dtype=F32), jnp.ones((HEAD_DIM, HEAD_DIM), F32)).astype(BF16)
    return pl.pallas_call(
        _mixer_kernel,
        grid=(BATCH, nb),
        in_specs=[
            pl.BlockSpec((BLOCK, ATTN_WIDTH), lambda b, n: (cur(b, n), 0)),
            pl.BlockSpec((BLOCK, KV_WIDTH), lambda b, n: (cur(b, n), Z_K_BLOCK)),
            pl.BlockSpec((BLOCK, KV_WIDTH), lambda b, n: (prev(b, n), Z_K_BLOCK)),
            pl.BlockSpec((BLOCK, KV_WIDTH), lambda b, n: (cur(b, n), Z_V_BLOCK)),
            pl.BlockSpec((BLOCK, KV_WIDTH), lambda b, n: (prev(b, n), Z_V_BLOCK)),
            pl.BlockSpec((CHUNK, SGU_HALF), lambda b, n: (cur(b, n), Z_U_BLOCK)),
            pl.BlockSpec((CHUNK, SGU_HALF), lambda b, n: (cur(b, n), Z_U_BLOCK + 1)),
            pl.BlockSpec((CHUNK, SGU_HALF), lambda b, n: (cur(b, n), Z_VS_BLOCK)),
            pl.BlockSpec((CHUNK, SGU_HALF), lambda b, n: (cur(b, n), Z_VS_BLOCK + 1)),
            pl.BlockSpec((None, N_KV_HEADS, 2 * BLOCK, 4 * BLOCK), lambda b, n: (jnp.minimum(n, 1), 0, 0, 0)),
            pl.BlockSpec(memory_space=pltpu.SMEM),
            whole(1, QGROUP),
            whole(1, QGROUP),
            whole(QGROUP, QGROUP),
            whole(1, SGU_WIDTH),
            whole(1, SGU_WIDTH),
            whole(SGU_GROUPS, CHUNK, CHUNK),
            whole(CHUNK, SGU_GROUPS),
        ] + cast_in,
        out_specs=[pl.BlockSpec((BLOCK, MIX_WIDTH), lambda b, n: (cur(b, n), 0))] + cast_out,
        out_shape=[jax.ShapeDtypeStruct((TOKENS, MIX_WIDTH), BF16)] + cast_shapes,
        compiler_params=_params(("arbitrary", "arbitrary"), 32 * MIB + cast_bytes),
        name="mixer",
    )(z, z, z, z, z, z, z, z, z, band_bias, sinks * LOG2E, q_gain, k_gain, head_ones,
      sgu_ln_w.reshape(1, SGU_WIDTH), sgu_ln_b.reshape(1, SGU_WIDTH), w_spatial, b_spatial.T, *cast_args)


def _mix_out_kernel(y_ref, w_ref, x_ref, mod_ref, o_ref):
    acc = jnp.dot(y_ref[...], w_ref[...], preferred_element_type=F32)
    o_ref[...] = x_ref[...] + mod_ref[2:3, :] * acc


def _mix_out(y, w_out, x, mod):
    tiles_per_seq = SEQ // MIX_OUT_ROWS
    blocks = (2 * MIX_OUT_ROWS * MIX_WIDTH * 2 + 2 * MIX_WIDTH * MIX_OUT_COLS * 2
              + 6 * MIX_OUT_ROWS * MIX_OUT_COLS * 4)
    return pl.pallas_call(
        _mix_out_kernel,
        grid=(TOKENS // MIX_OUT_ROWS, D_MODEL // MIX_OUT_COLS),
        in_specs=[
            pl.BlockSpec((MIX_OUT_ROWS, MIX_WIDTH), lambda i, j: (i, 0)),
            pl.BlockSpec((MIX_WIDTH, MIX_OUT_COLS), lambda i, j: (0, j)),
            pl.BlockSpec((MIX_OUT_ROWS, MIX_OUT_COLS), lambda i, j: (i, j)),
            pl.BlockSpec((None, N_MOD, MIX_OUT_COLS), lambda i, j: (i // tiles_per_seq, 0, j)),
        ],
        out_specs=pl.BlockSpec((MIX_OUT_ROWS, MIX_OUT_COLS), lambda i, j: (i, j)),
        out_shape=jax.ShapeDtypeStruct((TOKENS, D_MODEL), F32),
        compiler_params=_params(("arbitrary", "arbitrary"), blocks + 4 * MIB),
        name="mix_out",
    )(y, w_out, x, mod)


def _ffn_kernel(x_ref, xh_ref, mod_ref, nw_ref, wg_ref, wv_ref, cp_ref, wd_ref, o_ref, h_ref, a0_ref, a1_ref, g0_ref,
                g1_ref):
    i = pl.program_id(0)
    j = pl.program_id(1)
    a_refs, g_refs = (a0_ref, a1_ref), (g0_ref, g1_ref)

    gate_cols, val_cols = slice(0, FF_BLOCK), slice(FF_BLOCK, 2 * FF_BLOCK)

    def up(slot, rows=slice(None)):
        h = h_ref[rows, :]
        a_refs[slot][rows, gate_cols] = jnp.dot(h, wg_ref[...], preferred_element_type=F32)
        a_refs[slot][rows, val_cols] = jnp.dot(h, wv_ref[...], preferred_element_type=F32)

    def gate(slot, piece):
        a_ref, g_ref = a_refs[slot], g_refs[slot]
        rows = ROW_TILE // FFN_PIECES

        def conv(r0, cols, p0):
            lag = lambda k: a_ref[HALO - k + r0:HALO - k + r0 + GATE_ROWS, cols]
            return (lag(2) * cp_ref[p0:p0 + 1, :] + lag(1) * cp_ref[p0 + 1:p0 + 2, :]
                    + lag(0) * cp_ref[p0 + 2:p0 + 3, :] + cp_ref[p0 + 3:p0 + 4, :])

        for r0 in range(piece * rows, (piece + 1) * rows, GATE_ROWS):
            a_gate = conv(r0, gate_cols, 0)
            a_val = conv(r0, val_cols, CONV_WIDTH + 1)
            g_ref[r0:r0 + GATE_ROWS, :] = (a_gate * jax.nn.sigmoid(a_gate) * a_val).astype(BF16)

    def down(slot, piece):
        cols = slice(piece * (D_MODEL // FFN_PIECES), (piece + 1) * (D_MODEL // FFN_PIECES))
        return cols, jnp.dot(g_refs[slot][...], wd_ref[:, cols], preferred_element_type=F32)

    def gate_and_down(gate_slot, down_slot):
        for piece in range(FFN_PIECES):
            cols, acc = down(down_slot, piece)
            o_ref[:, cols] += acc
            gate(gate_slot, piece)

    last_slot = (N_FF - 1) % 2

    @pl.when(j == 0)
    def _():
        gain, shift = nw_ref[...] * (1.0 + mod_ref[4:5, :]), mod_ref[3:4, :]
        keep = jnp.where(i % (SEQ // ROW_TILE) == 0, 0.0, 1.0)
        h_ref[0:HALO, :] = (_adaln_rows(xh_ref[...], gain, shift) * keep).astype(BF16)
        o_ref[...] = jnp.zeros_like(o_ref)
        _norm_rows(x_ref, h_ref, HALO, 0, NORM_PIECE, gain, shift)
        for r0 in range(0, ROW_TILE, NORM_PIECE):
            up(0, slice(HALO + r0 if r0 else 0, HALO + r0 + NORM_PIECE))
            if r0 + NORM_PIECE < ROW_TILE:
                _norm_rows(x_ref, h_ref, HALO, r0 + NORM_PIECE, NORM_PIECE, gain, shift)

    @pl.when(j == 1)
    def _():
        up(1)
        for piece in range(FFN_PIECES):
            gate(0, piece)

    for slot in range(2):
        @pl.when((j >= 2) & (j < N_FF) & (lax.rem(j, 2) == slot))
        def _():
            gate_and_down(1 - slot, slot)
            up(slot)

    @pl.when(j == N_FF)
    def _():
        gate_and_down(last_slot, 1 - last_slot)

    @pl.when(j == N_FF + 1)
    def _():
        for piece in range(FFN_PIECES):
            cols, acc = down(last_slot, piece)
            o_ref[:, cols] = x_ref[:, cols] + mod_ref[5:6, cols] * (o_ref[:, cols] + acc)


def _ffn(x, mod, norm_w, w_up, conv_w, conv_b, w_down):
    tiles_per_seq = SEQ // ROW_TILE
    halo_blocks_per_tile = ROW_TILE // HALO
    conv_pack = jnp.concatenate([conv_w, conv_b.reshape(1, 2 * D_FF)], axis=0)
    conv_pack = conv_pack.reshape(CONV_WIDTH + 1, 2, N_FF, FF_BLOCK).transpose(2, 1, 0, 3)
    conv_pack = conv_pack.reshape(N_FF, 2 * (CONV_WIDTH + 1), FF_BLOCK)
    blk = lambda j, lag: jnp.clip(j - lag, 0, N_FF - 1)
    val = lambda b: N_FF + b
    ext_rows = HALO + ROW_TILE
    scratch = ext_rows * D_MODEL * 2 + 2 * ext_rows * 2 * FF_BLOCK * 4 + 2 * ROW_TILE * FF_BLOCK * 2
    blocks = 4 * ROW_TILE * D_MODEL * 4 + 6 * D_MODEL * FF_BLOCK * 2 + scratch + 6 * ext_rows * FF_BLOCK * 4
    return pl.pallas_call(
        _ffn_kernel,
        grid=(TOKENS // ROW_TILE, N_FF + 2),
        in_specs=[
            pl.BlockSpec((ROW_TILE, D_MODEL), lambda i, j: (i, 0)),
            pl.BlockSpec((HALO, D_MODEL), lambda i, j: (jnp.maximum(i * halo_blocks_per_tile - 1, 0), 0)),
            pl.BlockSpec((None, N_MOD, D_MODEL), lambda i, j: (i // tiles_per_seq, 0, 0)),
            pl.BlockSpec((1, D_MODEL), lambda i, j: (0, 0)),
            pl.BlockSpec((D_MODEL, FF_BLOCK), lambda i, j: (0, blk(j, 0))),
            pl.BlockSpec((D_MODEL, FF_BLOCK), lambda i, j: (0, val(blk(j, 0)))),
            pl.BlockSpec((None, 2 * (CONV_WIDTH + 1), FF_BLOCK), lambda i, j: (blk(j, 1), 0, 0)),
            pl.BlockSpec((FF_BLOCK, D_MODEL), lambda i, j: (blk(j, 2), 0)),
        ],
        out_specs=pl.BlockSpec((ROW_TILE, D_MODEL), lambda i, j: (i, 0)),
        out_shape=jax.ShapeDtypeStruct((TOKENS, D_MODEL), F32),
        scratch_shapes=[pltpu.VMEM((ext_rows, D_MODEL), BF16)] + 2 * [pltpu.VMEM((ext_rows, 2 * FF_BLOCK), F32)]
        + 2 * [pltpu.VMEM((ROW_TILE, FF_BLOCK), BF16)],
        compiler_params=_params(("arbitrary", "arbitrary"), blocks),
        name="ffn",
    )(x, x, mod, norm_w.reshape(1, D_MODEL), w_up, w_up, conv_pack, w_down)


def kernel(x, c, w_ada, b_ada, ada_table, rel_bias, norm_mix_w, w_in, q_norm_w, k_norm_w, sinks, sgu_ln_w,
           sgu_ln_b, w_spatial, b_spatial, w_out, norm_ffn_w, w_up, conv_w, conv_b, w_down):
    B, S, D = x.shape
    assert (B, S, D) == (BATCH, SEQ, D_MODEL)
    mods = _ada_mod(c, w_ada, b_ada, ada_table).reshape(DEPTH, BATCH, N_MOD, D_MODEL)
    band_bias = _band_bias(rel_bias)
    xt = x.reshape(TOKENS, D_MODEL)
    stacks = (w_in, w_out, w_up, w_down)
    w_in_l, w_out_l, w_up_l, w_down_l = (w[0].astype(BF16) for w in stacks)
    for l in range(DEPTH):
        mod = mods[l]
        z = _mix_in(xt, mod, norm_mix_w[l], w_in_l)
        y, *next_weights = _mixer(z, band_bias, sinks[l], q_norm_w[l], k_norm_w[l], sgu_ln_w[l], sgu_ln_b[l],
                                  w_spatial[l], b_spatial[l], cast=(stacks, l + 1) if l + 1 < DEPTH else None)
        xt = _mix_out(y, w_out_l, xt, mod)
        xt = _ffn(xt, mod, norm_ffn_w[l], w_up_l, conv_w[l], conv_b[l], w_down_l)
        if next_weights:
            w_in_l, w_out_l, w_up_l, w_down_l = next_weights
    return xt.reshape(B, S, D)
```

```python
import jax
import jax.numpy as jnp
import math
from jax import lax
from jax.experimental import pallas as pl
from jax.experimental.pallas import tpu as pltpu

D_MODEL = 4096
BATCH = 8
SEQ = 2048
DEPTH = 4
HEAD_DIM = 64
N_Q_HEADS = (D_MODEL // 2) // HEAD_DIM
N_KV_HEADS = 8
GQA_GROUP = N_Q_HEADS // N_KV_HEADS
ATTN_WIDTH = N_Q_HEADS * HEAD_DIM
KV_WIDTH = N_KV_HEADS * HEAD_DIM
WINDOW = 128
BLOCK = WINDOW
SGU_WIDTH = D_MODEL - ATTN_WIDTH
SGU_GROUPS = 8
SGU_GROUP_DIM = SGU_WIDTH // SGU_GROUPS
CHUNK = 128
MIX_WIDTH = ATTN_WIDTH + SGU_WIDTH
IN_WIDTH = ATTN_WIDTH + 2 * KV_WIDTH + 2 * SGU_WIDTH
D_FF = 11008
CONV_WIDTH = 3
N_BUCKETS = 32
MAX_DISTANCE = 128
N_MOD = 6
EPS = 1e-6
NEG_INF = -1e30
LOG2E = math.log2(math.e)

TOKENS = BATCH * SEQ
F32 = jnp.float32
BF16 = jnp.bfloat16

V7X_VMEM_BYTES = 64 * 2**20
V7X_LANES = 128
V7X_MXU_DIM = 256
BF16_SUBLANES = 16
MIB = 2**20

ROW_TILE = 512
NORM_CHUNK = 16
NORM_PIECE = 128
MIX_IN_COLS = 1024
MIX_OUT_ROWS = 1024
MIX_OUT_COLS = 512
FF_BLOCK = 256
N_FF = D_FF // FF_BLOCK
HALO = BF16_SUBLANES
GATE_ROWS = 32
FFN_PIECES = 8
ADA_COLS = 512
SGU_HALF = SGU_WIDTH // 2
PAIR = 2 * HEAD_DIM
QGROUP = GQA_GROUP * HEAD_DIM

assert PAIR == V7X_LANES and QGROUP == V7X_MXU_DIM and GQA_GROUP == 4
assert D_FF % FF_BLOCK == 0 and SEQ % ROW_TILE == 0 and SEQ % MIX_OUT_ROWS == 0


def _params(semantics, vmem_bytes):
    return pltpu.CompilerParams(dimension_semantics=semantics,
                                vmem_limit_bytes=int(min(V7X_VMEM_BYTES - 6 * MIB, vmem_bytes)))


def _ada_kernel(c_ref, w_ref, b_ref, tab_ref, o_ref):
    c = c_ref[...]
    a = c * jax.nn.sigmoid(c)
    mod = jnp.dot(a, w_ref[...], preferred_element_type=F32, precision=lax.Precision.HIGHEST) + b_ref[...]
    for l in range(DEPTH):
        o_ref[l] = mod + tab_ref[l:l + 1, :]


def _ada_mod(c, w_ada, b_ada, ada_table):
    width = N_MOD * D_MODEL
    return pl.pallas_call(
        _ada_kernel,
        grid=(width // ADA_COLS,),
        in_specs=[
            pl.BlockSpec((BATCH, D_MODEL), lambda j: (0, 0)),
            pl.BlockSpec((D_MODEL, ADA_COLS), lambda j: (0, j)),
            pl.BlockSpec((1, ADA_COLS), lambda j: (0, j)),
            pl.BlockSpec((DEPTH, ADA_COLS), lambda j: (0, j)),
        ],
        out_specs=pl.BlockSpec((DEPTH, BATCH, ADA_COLS), lambda j: (0, 0, j)),
        out_shape=jax.ShapeDtypeStruct((DEPTH, BATCH, width), F32),
        compiler_params=_params(("arbitrary",), 4 * D_MODEL * ADA_COLS * 4 + 8 * MIB),
        name="ada_mod",
    )(c, w_ada, b_ada.reshape(1, width), ada_table.reshape(DEPTH, width))


def _t5_causal_bucket(dist):
    max_exact = N_BUCKETS // 2
    n = jnp.maximum(dist, 0)
    nf = jnp.maximum(n, 1).astype(jnp.float32)
    large = max_exact + (jnp.log(nf / max_exact) / math.log(MAX_DISTANCE / max_exact)
                         * (N_BUCKETS - max_exact)).astype(jnp.int32)
    large = jnp.minimum(large, N_BUCKETS - 1)
    return jnp.where(n < max_exact, n, large)


def _bias_kernel(rb_ref, bucket_ref, o_ref):
    hk = pl.program_id(1)
    bucket = bucket_ref[...]
    for r in range(GQA_GROUP // 2):
        for par in range(2):
            hq = hk * GQA_GROUP + 2 * r + par
            acc = jnp.full(bucket.shape, NEG_INF, F32)
            for b in range(N_BUCKETS):
                acc = jnp.where(bucket == b, rb_ref[b, hq] * LOG2E, acc)
            o_ref[r * BLOCK:(r + 1) * BLOCK, par * 2 * BLOCK:(par + 1) * 2 * BLOCK] = acc


def _band_bias(rel_bias):
    qi = jnp.arange(BLOCK)[:, None]
    kj = jnp.arange(2 * BLOCK)[None, :]
    dist = qi + BLOCK - kj
    in_window = (dist >= 0) & (dist < WINDOW)
    bucket = jnp.where(in_window, _t5_causal_bucket(dist), -1).astype(jnp.int32)
    buckets = jnp.stack([jnp.where(kj >= BLOCK, bucket, -1), bucket])
    return pl.pallas_call(
        _bias_kernel,
        grid=(2, N_KV_HEADS),
        in_specs=[
            pl.BlockSpec(memory_space=pltpu.SMEM),
            pl.BlockSpec((None, BLOCK, 2 * BLOCK), lambda v, h: (v, 0, 0)),
        ],
        out_specs=pl.BlockSpec((None, None, 2 * BLOCK, 4 * BLOCK), lambda v, h: (v, h, 0, 0)),
        out_shape=jax.ShapeDtypeStruct((2, N_KV_HEADS, 2 * BLOCK, 4 * BLOCK), F32),
        name="band_bias",
    )(rel_bias, buckets)


def _adaln_rows(x, gain, shift):
    ms = jnp.mean(x * x, axis=-1, keepdims=True)
    return x * lax.rsqrt(ms + EPS) * gain + shift


def _norm_rows(x_ref, h_ref, h_row0, row0, rows, gain, shift):
    for r in range(row0, row0 + rows, NORM_CHUNK):
        h_ref[h_row0 + r:h_row0 + r + NORM_CHUNK, :] = _adaln_rows(x_ref[r:r + NORM_CHUNK, :], gain, shift).astype(BF16)


def _mix_in_kernel(x_ref, mod_ref, nw_ref, w_ref, o_ref, h_ref):
    j = pl.program_id(1)

    @pl.when(j == 0)
    def _():
        gain, shift = nw_ref[...] * (1.0 + mod_ref[1:2, :]), mod_ref[0:1, :]
        _norm_rows(x_ref, h_ref, 0, 0, NORM_PIECE, gain, shift)
        for r0 in range(0, ROW_TILE, NORM_PIECE):
            rows = slice(r0, r0 + NORM_PIECE)
            o_ref[rows, :] = jnp.dot(h_ref[rows, :], w_ref[...], preferred_element_type=F32).astype(o_ref.dtype)
            if r0 + NORM_PIECE < ROW_TILE:
                _norm_rows(x_ref, h_ref, 0, r0 + NORM_PIECE, NORM_PIECE, gain, shift)

    @pl.when(j > 0)
    def _():
        o_ref[...] = jnp.dot(h_ref[...], w_ref[...], preferred_element_type=F32).astype(o_ref.dtype)


def _mix_in(x, mod, norm_w, w_in):
    tiles_per_seq = SEQ // ROW_TILE
    blocks = (2 * ROW_TILE * D_MODEL * 4 + ROW_TILE * D_MODEL * 2
              + 2 * D_MODEL * MIX_IN_COLS * 2 + 4 * ROW_TILE * MIX_IN_COLS * 4)
    return pl.pallas_call(
        _mix_in_kernel,
        grid=(TOKENS // ROW_TILE, IN_WIDTH // MIX_IN_COLS),
        in_specs=[
            pl.BlockSpec((ROW_TILE, D_MODEL), lambda i, j: (i, 0)),
            pl.BlockSpec((None, N_MOD, D_MODEL), lambda i, j: (i // tiles_per_seq, 0, 0)),
            pl.BlockSpec((1, D_MODEL), lambda i, j: (0, 0)),
            pl.BlockSpec((D_MODEL, MIX_IN_COLS), lambda i, j: (0, j)),
        ],
        out_specs=pl.BlockSpec((ROW_TILE, MIX_IN_COLS), lambda i, j: (i, j)),
        out_shape=jax.ShapeDtypeStruct((TOKENS, IN_WIDTH), BF16),
        scratch_shapes=[pltpu.VMEM((ROW_TILE, D_MODEL), BF16)],
        compiler_params=_params(("arbitrary", "arbitrary"), blocks + 4 * MIB),
        name="mix_in",
    )(x, mod, norm_w.reshape(1, D_MODEL), w_in)


def _cast_plan(cast, n_steps, step_of):
    in_specs, out_specs, out_shapes, args, vmem_bytes = [], [], [], [], 0
    if cast is not None:
        weights, layer = cast
        for w in weights:
            _, rows, cols = w.shape
            chunk = next(r for r in range(BF16_SUBLANES, rows + 1, BF16_SUBLANES)
                         if rows % r == 0 and r * n_steps >= rows)
            last = rows // chunk - 1
            in_specs.append(pl.BlockSpec((None, chunk, cols),
                                         lambda *g, last=last: (layer, jnp.minimum(step_of(*g), last), 0)))
            out_specs.append(pl.BlockSpec((chunk, cols), lambda *g, last=last: (jnp.minimum(step_of(*g), last), 0)))
            out_shapes.append(jax.ShapeDtypeStruct((rows, cols), BF16))
            args.append(w)
            vmem_bytes += 2 * chunk * cols * (4 + 2)
    return in_specs, out_specs, out_shapes, args, vmem_bytes


def _split_cast_refs(rest):
    n = len(rest) // 2
    return rest[n], rest[:n], rest[n + 1:]


def _cast_chunks(src_refs, dst_refs):
    for src_ref, dst_ref in zip(src_refs, dst_refs):
        dst_ref[...] = src_ref[...].astype(dst_ref.dtype)


Z_K_BLOCK = ATTN_WIDTH // KV_WIDTH
Z_V_BLOCK = Z_K_BLOCK + 1
Z_U_BLOCK = (ATTN_WIDTH + 2 * KV_WIDTH) // SGU_HALF
Z_VS_BLOCK = Z_U_BLOCK + 2


def _grouped_rms(t, gain, head_ones):
    rows, width = t.shape
    n = width // QGROUP
    sq = t * t
    stacked = jnp.concatenate([sq[:, c * QGROUP:(c + 1) * QGROUP] for c in range(n)], axis=0).astype(BF16)
    ss = jnp.dot(stacked, head_ones, preferred_element_type=F32)
    rs = lax.rsqrt(ss * (1.0 / HEAD_DIM) + EPS)
    return [t[:, c * QGROUP:(c + 1) * QGROUP] * rs[c * rows:(c + 1) * rows] * gain for c in range(n)]


def _attention(q_ref, kc_ref, kp_ref, vc_ref, vp_ref, bias_ref, sinks_ref, qg_ref, kg_ref, ones_ref, o_ref):
    rows2 = 2 * BLOCK
    head_ones = ones_ref[...]
    lo = lax.broadcasted_iota(jnp.int32, (rows2, PAIR), 1) < HEAD_DIM
    top = lax.broadcasted_iota(jnp.int32, (rows2, 1), 0) < BLOCK
    ones_cols = jnp.concatenate([jnp.where(lo, 1.0, 0.0), jnp.where(lo, 0.0, 1.0)], axis=0).astype(BF16)

    qn = _grouped_rms(q_ref[...].astype(F32), qg_ref[...], head_ones)
    kn = _grouped_rms(jnp.concatenate([kp_ref[...], kc_ref[...]], axis=0).astype(F32), kg_ref[...], head_ones)
    vf = jnp.concatenate([vp_ref[...], vc_ref[...]], axis=0).astype(F32)

    def split_pair(pair):
        swapped = pltpu.roll(pair, HEAD_DIM, axis=1)
        a = jnp.concatenate([jnp.where(lo, pair, 0.0), jnp.where(lo, 0.0, swapped)], axis=0)
        b = jnp.concatenate([jnp.where(lo, swapped, 0.0), jnp.where(lo, 0.0, pair)], axis=0)
        return a.astype(BF16), b.astype(BF16)

    kk, vv = [], []
    for p in range(N_KV_HEADS // 2):
        kk.extend(split_pair(kn[p // 2][:, (p % 2) * PAIR:(p % 2 + 1) * PAIR]))
        vv.extend(split_pair(vf[:, p * PAIR:(p + 1) * PAIR]))

    def scores(hk):
        q2 = jnp.concatenate([qn[hk][:, :PAIR], qn[hk][:, PAIR:]], axis=0).astype(BF16)
        s = lax.dot_general(q2, kk[hk], (((1,), (1,)), ((), ())), preferred_element_type=F32)
        return s + bias_ref[hk]

    def softmax_pv(hk, s):
        probs, sink_terms = [], []
        for par in range(2):
            sp = s[:, par * rows2:(par + 1) * rows2]
            sink = jnp.where(top, sinks_ref[hk * GQA_GROUP + par], sinks_ref[hk * GQA_GROUP + 2 + par])
            m = jnp.maximum(jnp.max(sp, axis=-1, keepdims=True), sink)
            probs.append(jnp.exp2(sp - m).astype(BF16))
            sink_terms.append(jnp.exp2(sink - m))
        nd = jnp.dot(jnp.concatenate(probs, axis=1), jnp.concatenate([vv[hk], ones_cols], axis=1),
                     preferred_element_type=F32)
        den = nd[:, PAIR:] + jnp.where(lo, sink_terms[0], sink_terms[1])
        out = (nd[:, :PAIR] / den).astype(o_ref.dtype)
        o_ref[:, (2 * hk) * PAIR:(2 * hk + 1) * PAIR] = out[:BLOCK]
        o_ref[:, (2 * hk + 1) * PAIR:(2 * hk + 2) * PAIR] = out[BLOCK:]

    s_next = scores(0)
    for hk in range(N_KV_HEADS):
        s = s_next
        if hk + 1 < N_KV_HEADS:
            s_next = scores(hk + 1)
        softmax_pv(hk, s)


def _gelu(t):
    return 0.5 * t * (1.0 + lax.erf(t * (2.0 ** -0.5)))


def _sgu(u_refs, vs_refs, lnw_ref, lnb_ref, ws_ref, bst_ref, o_ref):
    v = _gelu(jnp.concatenate([r[...] for r in vs_refs], axis=1).astype(F32))
    mu = jnp.mean(v, axis=-1, keepdims=True)
    vc = v - mu
    var = jnp.mean(vc * vc, axis=-1, keepdims=True)
    vn = (vc * lax.rsqrt(var + EPS) * lnw_ref[...] + lnb_ref[...]).astype(BF16)
    row = lax.broadcasted_iota(jnp.int32, (CHUNK, CHUNK), 0)
    col = lax.broadcasted_iota(jnp.int32, (CHUNK, CHUNK), 1)
    causal = col <= row
    groups_per_ref = SGU_HALF // SGU_GROUP_DIM
    for g in range(SGU_GROUPS):
        ws = jnp.where(causal, ws_ref[g], 0.0).astype(BF16)
        mixed = (jnp.dot(ws, vn[:, g * SGU_GROUP_DIM:(g + 1) * SGU_GROUP_DIM], preferred_element_type=F32)
                 + bst_ref[:, g:g + 1])
        gr = g % groups_per_ref
        u = _gelu(u_refs[g // groups_per_ref][:, gr * SGU_GROUP_DIM:(gr + 1) * SGU_GROUP_DIM].astype(F32))
        o_ref[:, ATTN_WIDTH + g * SGU_GROUP_DIM:ATTN_WIDTH + (g + 1) * SGU_GROUP_DIM] = (
            u * mixed).astype(o_ref.dtype)


def _mixer_kernel(q_ref, kc_ref, kp_ref, vc_ref, vp_ref, u0_ref, u1_ref, vs0_ref, vs1_ref, bias_ref, sinks_ref,
                  qg_ref, kg_ref, ones_ref, lnw_ref, lnb_ref, ws_ref, bst_ref, *rest):
    o_ref, cast_src, cast_dst = _split_cast_refs(rest)
    _attention(q_ref, kc_ref, kp_ref, vc_ref, vp_ref, bias_ref, sinks_ref, qg_ref, kg_ref, ones_ref, o_ref)
    _sgu((u0_ref, u1_ref), (vs0_ref, vs1_ref), lnw_ref, lnb_ref, ws_ref, bst_ref, o_ref)
    _cast_chunks(cast_src, cast_dst)


def _mixer(z, band_bias, sinks, q_norm_w, k_norm_w, sgu_ln_w, sgu_ln_b, w_spatial, b_spatial, cast=None):
    nb = SEQ // BLOCK
    cast_in, cast_out, cast_shapes, cast_args, cast_bytes = _cast_plan(cast, BATCH * nb, lambda b, n: b * nb + n)
    cur = lambda b, n: b * nb + n
    prev = lambda b, n: b * nb + jnp.maximum(n - 1, 0)
    whole = lambda *shape: pl.BlockSpec(shape, lambda b, n: (0,) * len(shape))
    q_gain = jnp.tile(q_norm_w * (HEAD_DIM ** -0.5 * LOG2E), GQA_GROUP).reshape(1, QGROUP)
    k_gain = jnp.tile(k_norm_w, GQA_GROUP).reshape(1, QGROUP)
    head_ones = jnp.kron(jnp.eye(GQA_GROUP, dtype=F32), jnp.ones((HEAD_DIM, HEAD_DIM), F32)).astype(BF16)
    return pl.pallas_call(
        _mixer_kernel,
        grid=(BATCH, nb),
        in_specs=[
            pl.BlockSpec((BLOCK, ATTN_WIDTH), lambda b, n: (cur(b, n), 0)),
            pl.BlockSpec((BLOCK, KV_WIDTH), lambda b, n: (cur(b, n), Z_K_BLOCK)),
            pl.BlockSpec((BLOCK, KV_WIDTH), lambda b, n: (prev(b, n), Z_K_BLOCK)),
            pl.BlockSpec((BLOCK, KV_WIDTH), lambda b, n: (cur(b, n), Z_V_BLOCK)),
            pl.BlockSpec((BLOCK, KV_WIDTH), lambda b, n: (prev(b, n), Z_V_BLOCK)),
            pl.BlockSpec((CHUNK, SGU_HALF), lambda b, n: (cur(b, n), Z_U_BLOCK)),
            pl.BlockSpec((CHUNK, SGU_HALF), lambda b, n: (cur(b, n), Z_U_BLOCK + 1)),
            pl.BlockSpec((CHUNK, SGU_HALF), lambda b, n: (cur(b, n), Z_VS_BLOCK)),
            pl.BlockSpec((CHUNK, SGU_HALF), lambda b, n: (cur(b, n), Z_VS_BLOCK + 1)),
            pl.BlockSpec((None, N_KV_HEADS, 2 * BLOCK, 4 * BLOCK), lambda b, n: (jnp.minimum(n, 1), 0, 0, 0)),
            pl.BlockSpec(memory_space=pltpu.SMEM),
            whole(1, QGROUP),
            whole(1, QGROUP),
            whole(QGROUP, QGROUP),
            whole(1, SGU_WIDTH),
            whole(1, SGU_WIDTH),
            whole(SGU_GROUPS, CHUNK, CHUNK),
            whole(CHUNK, SGU_GROUPS),
        ] + cast_in,
        out_specs=[pl.BlockSpec((BLOCK, MIX_WIDTH), lambda b, n: (cur(b, n), 0))] + cast_out,
        out_shape=[jax.ShapeDtypeStruct((TOKENS, MIX_WIDTH), BF16)] + cast_shapes,
        compiler_params=_params(("arbitrary", "arbitrary"), 32 * MIB + cast_bytes),
        name="mixer",
    )(z, z, z, z, z, z, z, z, z, band_bias, sinks * LOG2E, q_gain, k_gain, head_ones,
      sgu_ln_w.reshape(1, SGU_WIDTH), sgu_ln_b.reshape(1, SGU_WIDTH), w_spatial, b_spatial.T, *cast_args)


def _mix_out_kernel(y_ref, w_ref, x_ref, mod_ref, *rest):
    o_ref, cast_src, cast_dst = _split_cast_refs(rest)
    acc = jnp.dot(y_ref[...], w_ref[...], preferred_element_type=F32)
    o_ref[...] = x_ref[...] + mod_ref[2:3, :] * acc
    _cast_chunks(cast_src, cast_dst)


def _mix_out(y, w_out, x, mod, cast=None):
    tiles_per_seq = SEQ // MIX_OUT_ROWS
    col_steps = D_MODEL // MIX_OUT_COLS
    cast_in, cast_out, cast_shapes, cast_args, cast_bytes = _cast_plan(
        cast, TOKENS // MIX_OUT_ROWS * col_steps, lambda i, j: i * col_steps + j)
    blocks = (2 * MIX_OUT_ROWS * MIX_WIDTH * 2 + 2 * MIX_WIDTH * MIX_OUT_COLS * 2
              + 6 * MIX_OUT_ROWS * MIX_OUT_COLS * 4)
    return pl.pallas_call(
        _mix_out_kernel,
        grid=(TOKENS // MIX_OUT_ROWS, D_MODEL // MIX_OUT_COLS),
        in_specs=[
            pl.BlockSpec((MIX_OUT_ROWS, MIX_WIDTH), lambda i, j: (i, 0)),
            pl.BlockSpec((MIX_WIDTH, MIX_OUT_COLS), lambda i, j: (0, j)),
            pl.BlockSpec((MIX_OUT_ROWS, MIX_OUT_COLS), lambda i, j: (i, j)),
            pl.BlockSpec((None, N_MOD, MIX_OUT_COLS), lambda i, j: (i // tiles_per_seq, 0, j)),
        ] + cast_in,
        out_specs=[pl.BlockSpec((MIX_OUT_ROWS, MIX_OUT_COLS), lambda i, j: (i, j))] + cast_out,
        out_shape=[jax.ShapeDtypeStruct((TOKENS, D_MODEL), F32)] + cast_shapes,
        compiler_params=_params(("arbitrary", "arbitrary"), blocks + 4 * MIB + cast_bytes),
        name="mix_out",
    )(y, w_out, x, mod, *cast_args)


def _ffn_kernel(x_ref, xh_ref, mod_ref, nw_ref, wg_ref, wv_ref, cp_ref, wd_ref, o_ref, h_ref, a0_ref, a1_ref, g0_ref,
                g1_ref):
    i = pl.program_id(0)
    j = pl.program_id(1)
    a_refs, g_refs = (a0_ref, a1_ref), (g0_ref, g1_ref)

    gate_cols, val_cols = slice(0, FF_BLOCK), slice(FF_BLOCK, 2 * FF_BLOCK)

    def up(slot, rows=slice(None)):
        h = h_ref[rows, :]
        a_refs[slot][rows, gate_cols] = jnp.dot(h, wg_ref[...], preferred_element_type=F32)
        a_refs[slot][rows, val_cols] = jnp.dot(h, wv_ref[...], preferred_element_type=F32)

    def gate(slot, piece):
        a_ref, g_ref = a_refs[slot], g_refs[slot]
        rows = ROW_TILE // FFN_PIECES

        def conv(r0, cols, p0):
            lag = lambda k: a_ref[HALO - k + r0:HALO - k + r0 + GATE_ROWS, cols]
            return (lag(2) * cp_ref[p0:p0 + 1, :] + lag(1) * cp_ref[p0 + 1:p0 + 2, :]
                    + lag(0) * cp_ref[p0 + 2:p0 + 3, :] + cp_ref[p0 + 3:p0 + 4, :])

        for r0 in range(piece * rows, (piece + 1) * rows, GATE_ROWS):
            a_gate = conv(r0, gate_cols, 0)
            a_val = conv(r0, val_cols, CONV_WIDTH + 1)
            g_ref[r0:r0 + GATE_ROWS, :] = (a_gate * jax.nn.sigmoid(a_gate) * a_val).astype(BF16)

    def down(slot, piece):
        cols = slice(piece * (D_MODEL // FFN_PIECES), (piece + 1) * (D_MODEL // FFN_PIECES))
        return cols, jnp.dot(g_refs[slot][...], wd_ref[:, cols], preferred_element_type=F32)

    def gate_and_down(gate_slot, down_slot):
        for piece in range(FFN_PIECES):
            cols, acc = down(down_slot, piece)
            o_ref[:, cols] += acc
            gate(gate_slot, piece)

    last_slot = (N_FF - 1) % 2

    @pl.when(j == 0)
    def _():
        gain, shift = nw_ref[...] * (1.0 + mod_ref[4:5, :]), mod_ref[3:4, :]
        keep = jnp.where(i % (SEQ // ROW_TILE) == 0, 0.0, 1.0)
        h_ref[0:HALO, :] = (_adaln_rows(xh_ref[...], gain, shift) * keep).astype(BF16)
        o_ref[...] = jnp.zeros_like(o_ref)
        _norm_rows(x_ref, h_ref, HALO, 0, NORM_PIECE, gain, shift)
        for r0 in range(0, ROW_TILE, NORM_PIECE):
            up(0, slice(HALO + r0 if r0 else 0, HALO + r0 + NORM_PIECE))
            if r0 + NORM_PIECE < ROW_TILE:
                _norm_rows(x_ref, h_ref, HALO, r0 + NORM_PIECE, NORM_PIECE, gain, shift)

    @pl.when(j == 1)
    def _():
        up(1)
        for piece in range(FFN_PIECES):
            gate(0, piece)

    for slot in range(2):
        @pl.when((j >= 2) & (j < N_FF) & (lax.rem(j, 2) == slot))
        def _():
            gate_and_down(1 - slot, slot)
            up(slot)

    @pl.when(j == N_FF)
    def _():
        gate_and_down(last_slot, 1 - last_slot)

    @pl.when(j == N_FF + 1)
    def _():
        for piece in range(FFN_PIECES):
            cols, acc = down(last_slot, piece)
            o_ref[:, cols] = x_ref[:, cols] + mod_ref[5:6, cols] * (o_ref[:, cols] + acc)


def _ffn(x, mod, norm_w, w_up, conv_w, conv_b, w_down):
    tiles_per_seq = SEQ // ROW_TILE
    halo_blocks_per_tile = ROW_TILE // HALO
    conv_pack = jnp.concatenate([conv_w, conv_b.reshape(1, 2 * D_FF)], axis=0)
    conv_pack = conv_pack.reshape(CONV_WIDTH + 1, 2, N_FF, FF_BLOCK).transpose(2, 1, 0, 3)
    conv_pack = conv_pack.reshape(N_FF, 2 * (CONV_WIDTH + 1), FF_BLOCK)
    blk = lambda j, lag: jnp.clip(j - lag, 0, N_FF - 1)
    val = lambda b: N_FF + b
    ext_rows = HALO + ROW_TILE
    scratch = ext_rows * D_MODEL * 2 + 2 * ext_rows * 2 * FF_BLOCK * 4 + 2 * ROW_TILE * FF_BLOCK * 2
    blocks = 4 * ROW_TILE * D_MODEL * 4 + 6 * D_MODEL * FF_BLOCK * 2 + scratch + 6 * ext_rows * FF_BLOCK * 4
    return pl.pallas_call(
        _ffn_kernel,
        grid=(TOKENS // ROW_TILE, N_FF + 2),
        in_specs=[
            pl.BlockSpec((ROW_TILE, D_MODEL), lambda i, j: (i, 0)),
            pl.BlockSpec((HALO, D_MODEL), lambda i, j: (jnp.maximum(i * halo_blocks_per_tile - 1, 0), 0)),
            pl.BlockSpec((None, N_MOD, D_MODEL), lambda i, j: (i // tiles_per_seq, 0, 0)),
            pl.BlockSpec((1, D_MODEL), lambda i, j: (0, 0)),
            pl.BlockSpec((D_MODEL, FF_BLOCK), lambda i, j: (0, blk(j, 0))),
            pl.BlockSpec((D_MODEL, FF_BLOCK), lambda i, j: (0, val(blk(j, 0)))),
            pl.BlockSpec((None, 2 * (CONV_WIDTH + 1), FF_BLOCK), lambda i, j: (blk(j, 1), 0, 0)),
            pl.BlockSpec((FF_BLOCK, D_MODEL), lambda i, j: (blk(j, 2), 0)),
        ],
        out_specs=pl.BlockSpec((ROW_TILE, D_MODEL), lambda i, j: (i, 0)),
        out_shape=jax.ShapeDtypeStruct((TOKENS, D_MODEL), F32),
        scratch_shapes=[pltpu.VMEM((ext_rows, D_MODEL), BF16)] + 2 * [pltpu.VMEM((ext_rows, 2 * FF_BLOCK), F32)]
        + 2 * [pltpu.VMEM((ROW_TILE, FF_BLOCK), BF16)],
        compiler_params=_params(("arbitrary", "arbitrary"), blocks),
        name="ffn",
    )(x, x, mod, norm_w.reshape(1, D_MODEL), w_up, w_up, conv_pack, w_down)


def kernel(x, c, w_ada, b_ada, ada_table, rel_bias, norm_mix_w, w_in, q_norm_w, k_norm_w, sinks, sgu_ln_w,
           sgu_ln_b, w_spatial, b_spatial, w_out, norm_ffn_w, w_up, conv_w, conv_b, w_down):
    B, S, D = x.shape
    assert (B, S, D) == (BATCH, SEQ, D_MODEL)
    mods = _ada_mod(c, w_ada, b_ada, ada_table).reshape(DEPTH, BATCH, N_MOD, D_MODEL)
    band_bias = _band_bias(rel_bias)
    xt = x.reshape(TOKENS, D_MODEL)
    w_in_l, w_out_l, w_up_l, w_down_l = (w[0].astype(BF16) for w in (w_in, w_out, w_up, w_down))
    for l in range(DEPTH):
        mod, more = mods[l], l + 1 < DEPTH
        z = _mix_in(xt, mod, norm_mix_w[l], w_in_l)
        y, *cast_a = _mixer(z, band_bias, sinks[l], q_norm_w[l], k_norm_w[l], sgu_ln_w[l], sgu_ln_b[l],
                            w_spatial[l], b_spatial[l], cast=((w_in, w_up), l + 1) if more else None)
        xt, *cast_b = _mix_out(y, w_out_l, xt, mod, cast=((w_out, w_down), l + 1) if more else None)
        xt = _ffn(xt, mod, norm_ffn_w[l], w_up_l, conv_w[l], conv_b[l], w_down_l)
        if more:
            (w_in_l, w_up_l), (w_out_l, w_down_l) = cast_a, cast_b
    return xt.reshape(B, S, D)
```

```python
import jax
import jax.numpy as jnp
import math
from jax import lax
from jax.experimental import pallas as pl
from jax.experimental.pallas import tpu as pltpu

D_MODEL = 4096
BATCH = 8
SEQ = 2048
DEPTH = 4
HEAD_DIM = 64
N_Q_HEADS = (D_MODEL // 2) // HEAD_DIM
N_KV_HEADS = 8
GQA_GROUP = N_Q_HEADS // N_KV_HEADS
ATTN_WIDTH = N_Q_HEADS * HEAD_DIM
KV_WIDTH = N_KV_HEADS * HEAD_DIM
WINDOW = 128
BLOCK = WINDOW
SGU_WIDTH = D_MODEL - ATTN_WIDTH
SGU_GROUPS = 8
SGU_GROUP_DIM = SGU_WIDTH // SGU_GROUPS
CHUNK = 128
MIX_WIDTH = ATTN_WIDTH + SGU_WIDTH
IN_WIDTH = ATTN_WIDTH + 2 * KV_WIDTH + 2 * SGU_WIDTH
D_FF = 11008
CONV_WIDTH = 3
N_BUCKETS = 32
MAX_DISTANCE = 128
N_MOD = 6
EPS = 1e-6
NEG_INF = -1e30
LOG2E = math.log2(math.e)

TOKENS = BATCH * SEQ
F32 = jnp.float32
BF16 = jnp.bfloat16

V7X_VMEM_BYTES = 64 * 2**20
V7X_LANES = 128
V7X_MXU_DIM = 256
BF16_SUBLANES = 16
MIB = 2**20

ROW_TILE = 512
NORM_CHUNK = 16
NORM_PIECE = 128
MIX_IN_COLS = 1024
MIX_OUT_ROWS = 1024
MIX_OUT_COLS = 512
FF_BLOCK = 256
N_FF = D_FF // FF_BLOCK
HALO = BF16_SUBLANES
GATE_ROWS = 32
FFN_PIECES = 8
FFN_SCRATCH = 5
ADA_COLS = 512
SGU_HALF = SGU_WIDTH // 2
PAIR = 2 * HEAD_DIM
QGROUP = GQA_GROUP * HEAD_DIM

assert PAIR == V7X_LANES and QGROUP == V7X_MXU_DIM and GQA_GROUP == 4
assert D_FF % FF_BLOCK == 0 and SEQ % ROW_TILE == 0 and SEQ % MIX_OUT_ROWS == 0


def _params(semantics, vmem_bytes):
    return pltpu.CompilerParams(dimension_semantics=semantics,
                                vmem_limit_bytes=int(min(V7X_VMEM_BYTES - 6 * MIB, vmem_bytes)))


def _ada_kernel(c_ref, w_ref, b_ref, tab_ref, o_ref):
    c = c_ref[...]
    a = c * jax.nn.sigmoid(c)
    w = w_ref[...]
    a_hi, w_hi = a.astype(BF16), w.astype(BF16)
    a_lo, w_lo = (a - a_hi.astype(F32)).astype(BF16), (w - w_hi.astype(F32)).astype(BF16)
    dot = lambda p, q: jnp.dot(p, q, preferred_element_type=F32)
    mod = dot(a_hi, w_hi) + dot(a_lo, w_hi) + dot(a_hi, w_lo) + b_ref[...]
    for l in range(DEPTH):
        o_ref[l] = mod + tab_ref[l:l + 1, :]


def _ada_mod(c, w_ada, b_ada, ada_table):
    width = N_MOD * D_MODEL
    return pl.pallas_call(
        _ada_kernel,
        grid=(width // ADA_COLS,),
        in_specs=[
            pl.BlockSpec((BATCH, D_MODEL), lambda j: (0, 0)),
            pl.BlockSpec((D_MODEL, ADA_COLS), lambda j: (0, j)),
            pl.BlockSpec((1, ADA_COLS), lambda j: (0, j)),
            pl.BlockSpec((DEPTH, ADA_COLS), lambda j: (0, j)),
        ],
        out_specs=pl.BlockSpec((DEPTH, BATCH, ADA_COLS), lambda j: (0, 0, j)),
        out_shape=jax.ShapeDtypeStruct((DEPTH, BATCH, width), F32),
        compiler_params=_params(("arbitrary",), 4 * D_MODEL * ADA_COLS * 4 + 8 * MIB),
        name="ada_mod",
    )(c, w_ada, b_ada.reshape(1, width), ada_table.reshape(DEPTH, width))


def _t5_causal_bucket(dist):
    max_exact = N_BUCKETS // 2
    n = jnp.maximum(dist, 0)
    nf = jnp.maximum(n, 1).astype(jnp.float32)
    large = max_exact + (jnp.log(nf / max_exact) / math.log(MAX_DISTANCE / max_exact)
                         * (N_BUCKETS - max_exact)).astype(jnp.int32)
    large = jnp.minimum(large, N_BUCKETS - 1)
    return jnp.where(n < max_exact, n, large)


def _bias_kernel(rb_ref, bucket_ref, o_ref):
    hk = pl.program_id(1)
    bucket = bucket_ref[...]
    for r in range(GQA_GROUP // 2):
        for par in range(2):
            hq = hk * GQA_GROUP + 2 * r + par
            acc = jnp.full(bucket.shape, NEG_INF, F32)
            for b in range(N_BUCKETS):
                acc = jnp.where(bucket == b, rb_ref[b, hq] * LOG2E, acc)
            o_ref[r * BLOCK:(r + 1) * BLOCK, par * 2 * BLOCK:(par + 1) * 2 * BLOCK] = acc


def _band_bias(rel_bias):
    qi = jnp.arange(BLOCK)[:, None]
    kj = jnp.arange(2 * BLOCK)[None, :]
    dist = qi + BLOCK - kj
    in_window = (dist >= 0) & (dist < WINDOW)
    bucket = jnp.where(in_window, _t5_causal_bucket(dist), -1).astype(jnp.int32)
    buckets = jnp.stack([jnp.where(kj >= BLOCK, bucket, -1), bucket])
    return pl.pallas_call(
        _bias_kernel,
        grid=(2, N_KV_HEADS),
        in_specs=[
            pl.BlockSpec(memory_space=pltpu.SMEM),
            pl.BlockSpec((None, BLOCK, 2 * BLOCK), lambda v, h: (v, 0, 0)),
        ],
        out_specs=pl.BlockSpec((None, None, 2 * BLOCK, 4 * BLOCK), lambda v, h: (v, h, 0, 0)),
        out_shape=jax.ShapeDtypeStruct((2, N_KV_HEADS, 2 * BLOCK, 4 * BLOCK), F32),
        name="band_bias",
    )(rel_bias, buckets)


def _adaln_rows(x, gain, shift):
    ms = jnp.mean(x * x, axis=-1, keepdims=True)
    return x * lax.rsqrt(ms + EPS) * gain + shift


def _norm_rows(x_ref, h_ref, h_row0, row0, rows, gain, shift):
    for r in range(row0, row0 + rows, NORM_CHUNK):
        h_ref[h_row0 + r:h_row0 + r + NORM_CHUNK, :] = _adaln_rows(x_ref[r:r + NORM_CHUNK, :], gain, shift).astype(BF16)


def _mix_in_kernel(x_ref, mod_ref, nw_ref, w_ref, o_ref, h_ref):
    j = pl.program_id(1)

    @pl.when(j == 0)
    def _():
        gain, shift = nw_ref[...] * (1.0 + mod_ref[1:2, :]), mod_ref[0:1, :]
        _norm_rows(x_ref, h_ref, 0, 0, NORM_PIECE, gain, shift)
        for r0 in range(0, ROW_TILE, NORM_PIECE):
            rows = slice(r0, r0 + NORM_PIECE)
            o_ref[rows, :] = jnp.dot(h_ref[rows, :], w_ref[...], preferred_element_type=F32).astype(o_ref.dtype)
            if r0 + NORM_PIECE < ROW_TILE:
                _norm_rows(x_ref, h_ref, 0, r0 + NORM_PIECE, NORM_PIECE, gain, shift)

    @pl.when(j > 0)
    def _():
        o_ref[...] = jnp.dot(h_ref[...], w_ref[...], preferred_element_type=F32).astype(o_ref.dtype)


def _mix_in(x, mod, norm_w, w_in):
    tiles_per_seq = SEQ // ROW_TILE
    blocks = (2 * ROW_TILE * D_MODEL * 4 + ROW_TILE * D_MODEL * 2
              + 2 * D_MODEL * MIX_IN_COLS * 2 + 4 * ROW_TILE * MIX_IN_COLS * 4)
    return pl.pallas_call(
        _mix_in_kernel,
        grid=(TOKENS // ROW_TILE, IN_WIDTH // MIX_IN_COLS),
        in_specs=[
            pl.BlockSpec((ROW_TILE, D_MODEL), lambda i, j: (i, 0)),
            pl.BlockSpec((None, N_MOD, D_MODEL), lambda i, j: (i // tiles_per_seq, 0, 0)),
            pl.BlockSpec((1, D_MODEL), lambda i, j: (0, 0)),
            pl.BlockSpec((D_MODEL, MIX_IN_COLS), lambda i, j: (0, j)),
        ],
        out_specs=pl.BlockSpec((ROW_TILE, MIX_IN_COLS), lambda i, j: (i, j)),
        out_shape=jax.ShapeDtypeStruct((TOKENS, IN_WIDTH), BF16),
        scratch_shapes=[pltpu.VMEM((ROW_TILE, D_MODEL), BF16)],
        compiler_params=_params(("arbitrary", "arbitrary"), blocks + 4 * MIB),
        name="mix_in",
    )(x, mod, norm_w.reshape(1, D_MODEL), w_in)


def _cast_plan(cast, n_steps, step_of):
    in_specs, out_specs, out_shapes, args, vmem_bytes = [], [], [], [], 0
    if cast is not None:
        weights, layer = cast
        for w in weights:
            _, rows, cols = w.shape
            chunk = next(r for r in range(BF16_SUBLANES, rows + 1, BF16_SUBLANES)
                         if rows % r == 0 and r * n_steps >= rows)
            last = rows // chunk - 1
            in_specs.append(pl.BlockSpec((None, chunk, cols),
                                         lambda *g, last=last: (layer, jnp.minimum(step_of(*g), last), 0)))
            out_specs.append(pl.BlockSpec((chunk, cols), lambda *g, last=last: (jnp.minimum(step_of(*g), last), 0)))
            out_shapes.append(jax.ShapeDtypeStruct((rows, cols), BF16))
            args.append(w)
            vmem_bytes += 2 * chunk * cols * (4 + 2)
    return in_specs, out_specs, out_shapes, args, vmem_bytes


def _split_cast_refs(rest):
    n = len(rest) // 2
    return rest[n], rest[:n], rest[n + 1:]


def _cast_chunks(src_refs, dst_refs):
    for src_ref, dst_ref in zip(src_refs, dst_refs):
        dst_ref[...] = src_ref[...].astype(dst_ref.dtype)


Z_K_BLOCK = ATTN_WIDTH // KV_WIDTH
Z_V_BLOCK = Z_K_BLOCK + 1
Z_U_BLOCK = (ATTN_WIDTH + 2 * KV_WIDTH) // SGU_HALF
Z_VS_BLOCK = Z_U_BLOCK + 2


def _grouped_rms(t, gain, head_ones):
    rows, width = t.shape
    n = width // QGROUP
    sq = t * t
    stacked = jnp.concatenate([sq[:, c * QGROUP:(c + 1) * QGROUP] for c in range(n)], axis=0).astype(BF16)
    ss = jnp.dot(stacked, head_ones, preferred_element_type=F32)
    rs = lax.rsqrt(ss * (1.0 / HEAD_DIM) + EPS)
    return [t[:, c * QGROUP:(c + 1) * QGROUP] * rs[c * rows:(c + 1) * rows] * gain for c in range(n)]


def _attention(q_ref, kc_ref, kp_ref, vc_ref, vp_ref, bias_ref, sinks_ref, qg_ref, kg_ref, ones_ref, o_ref):
    rows2 = 2 * BLOCK
    head_ones = ones_ref[...]
    lo = lax.broadcasted_iota(jnp.int32, (rows2, PAIR), 1) < HEAD_DIM
    top = lax.broadcasted_iota(jnp.int32, (rows2, 1), 0) < BLOCK
    ones_cols = jnp.concatenate([jnp.where(lo, 1.0, 0.0), jnp.where(lo, 0.0, 1.0)], axis=0).astype(BF16)

    qn = _grouped_rms(q_ref[...].astype(F32), qg_ref[...], head_ones)
    kn = _grouped_rms(jnp.concatenate([kp_ref[...], kc_ref[...]], axis=0).astype(F32), kg_ref[...], head_ones)
    vf = jnp.concatenate([vp_ref[...], vc_ref[...]], axis=0).astype(F32)

    def split_pair(pair):
        swapped = pltpu.roll(pair, HEAD_DIM, axis=1)
        a = jnp.concatenate([jnp.where(lo, pair, 0.0), jnp.where(lo, 0.0, swapped)], axis=0)
        b = jnp.concatenate([jnp.where(lo, swapped, 0.0), jnp.where(lo, 0.0, pair)], axis=0)
        return a.astype(BF16), b.astype(BF16)

    kk, vv = [], []
    for p in range(N_KV_HEADS // 2):
        kk.extend(split_pair(kn[p // 2][:, (p % 2) * PAIR:(p % 2 + 1) * PAIR]))
        vv.extend(split_pair(vf[:, p * PAIR:(p + 1) * PAIR]))

    def scores(hk):
        q2 = jnp.concatenate([qn[hk][:, :PAIR], qn[hk][:, PAIR:]], axis=0).astype(BF16)
        s = lax.dot_general(q2, kk[hk], (((1,), (1,)), ((), ())), preferred_element_type=F32)
        return s + bias_ref[hk]

    def softmax_pv(hk, s):
        probs, sink_terms = [], []
        for par in range(2):
            sp = s[:, par * rows2:(par + 1) * rows2]
            sink = jnp.where(top, sinks_ref[hk * GQA_GROUP + par], sinks_ref[hk * GQA_GROUP + 2 + par])
            m = jnp.maximum(jnp.max(sp, axis=-1, keepdims=True), sink)
            probs.append(jnp.exp2(sp - m).astype(BF16))
            sink_terms.append(jnp.exp2(sink - m))
        nd = jnp.dot(jnp.concatenate(probs, axis=1), jnp.concatenate([vv[hk], ones_cols], axis=1),
                     preferred_element_type=F32)
        den = nd[:, PAIR:] + jnp.where(lo, sink_terms[0], sink_terms[1])
        out = (nd[:, :PAIR] / den).astype(o_ref.dtype)
        o_ref[:, (2 * hk) * PAIR:(2 * hk + 1) * PAIR] = out[:BLOCK]
        o_ref[:, (2 * hk + 1) * PAIR:(2 * hk + 2) * PAIR] = out[BLOCK:]

    s_next = scores(0)
    for hk in range(N_KV_HEADS):
        s = s_next
        if hk + 1 < N_KV_HEADS:
            s_next = scores(hk + 1)
        softmax_pv(hk, s)


def _gelu(t):
    return 0.5 * t * (1.0 + lax.erf(t * (2.0 ** -0.5)))


def _sgu(u_refs, vs_refs, lnw_ref, lnb_ref, ws_ref, bst_ref, o_ref):
    v = _gelu(jnp.concatenate([r[...] for r in vs_refs], axis=1).astype(F32))
    mu = jnp.mean(v, axis=-1, keepdims=True)
    vc = v - mu
    var = jnp.mean(vc * vc, axis=-1, keepdims=True)
    vn = (vc * lax.rsqrt(var + EPS) * lnw_ref[...] + lnb_ref[...]).astype(BF16)
    row = lax.broadcasted_iota(jnp.int32, (CHUNK, CHUNK), 0)
    col = lax.broadcasted_iota(jnp.int32, (CHUNK, CHUNK), 1)
    causal = col <= row
    groups_per_ref = SGU_HALF // SGU_GROUP_DIM
    for g in range(SGU_GROUPS):
        ws = jnp.where(causal, ws_ref[g], 0.0).astype(BF16)
        mixed = (jnp.dot(ws, vn[:, g * SGU_GROUP_DIM:(g + 1) * SGU_GROUP_DIM], preferred_element_type=F32)
                 + bst_ref[:, g:g + 1])
        gr = g % groups_per_ref
        u = _gelu(u_refs[g // groups_per_ref][:, gr * SGU_GROUP_DIM:(gr + 1) * SGU_GROUP_DIM].astype(F32))
        o_ref[:, ATTN_WIDTH + g * SGU_GROUP_DIM:ATTN_WIDTH + (g + 1) * SGU_GROUP_DIM] = (
            u * mixed).astype(o_ref.dtype)


def _mixer_kernel(q_ref, kc_ref, kp_ref, vc_ref, vp_ref, u0_ref, u1_ref, vs0_ref, vs1_ref, bias_ref, sinks_ref,
                  qg_ref, kg_ref, ones_ref, lnw_ref, lnb_ref, ws_ref, bst_ref, *rest):
    o_ref, cast_src, cast_dst = _split_cast_refs(rest)
    _attention(q_ref, kc_ref, kp_ref, vc_ref, vp_ref, bias_ref, sinks_ref, qg_ref, kg_ref, ones_ref, o_ref)
    _sgu((u0_ref, u1_ref), (vs0_ref, vs1_ref), lnw_ref, lnb_ref, ws_ref, bst_ref, o_ref)
    _cast_chunks(cast_src, cast_dst)


def _mixer(z, band_bias, sinks, q_norm_w, k_norm_w, sgu_ln_w, sgu_ln_b, w_spatial, b_spatial, cast=None):
    nb = SEQ // BLOCK
    cast_in, cast_out, cast_shapes, cast_args, cast_bytes = _cast_plan(cast, BATCH * nb, lambda b, n: b * nb + n)
    cur = lambda b, n: b * nb + n
    prev = lambda b, n: b * nb + jnp.maximum(n - 1, 0)
    whole = lambda *shape: pl.BlockSpec(shape, lambda b, n: (0,) * len(shape))
    q_gain = jnp.tile(q_norm_w * (HEAD_DIM ** -0.5 * LOG2E), GQA_GROUP).reshape(1, QGROUP)
    k_gain = jnp.tile(k_norm_w, GQA_GROUP).reshape(1, QGROUP)
    head_ones = jnp.kron(jnp.eye(GQA_GROUP, dtype=F32), jnp.ones((HEAD_DIM, HEAD_DIM), F32)).astype(BF16)
    return pl.pallas_call(
        _mixer_kernel,
        grid=(BATCH, nb),
        in_specs=[
            pl.BlockSpec((BLOCK, ATTN_WIDTH), lambda b, n: (cur(b, n), 0)),
            pl.BlockSpec((BLOCK, KV_WIDTH), lambda b, n: (cur(b, n), Z_K_BLOCK)),
            pl.BlockSpec((BLOCK, KV_WIDTH), lambda b, n: (prev(b, n), Z_K_BLOCK)),
            pl.BlockSpec((BLOCK, KV_WIDTH), lambda b, n: (cur(b, n), Z_V_BLOCK)),
            pl.BlockSpec((BLOCK, KV_WIDTH), lambda b, n: (prev(b, n), Z_V_BLOCK)),
            pl.BlockSpec((CHUNK, SGU_HALF), lambda b, n: (cur(b, n), Z_U_BLOCK)),
            pl.BlockSpec((CHUNK, SGU_HALF), lambda b, n: (cur(b, n), Z_U_BLOCK + 1)),
            pl.BlockSpec((CHUNK, SGU_HALF), lambda b, n: (cur(b, n), Z_VS_BLOCK)),
            pl.BlockSpec((CHUNK, SGU_HALF), lambda b, n: (cur(b, n), Z_VS_BLOCK + 1)),
            pl.BlockSpec((None, N_KV_HEADS, 2 * BLOCK, 4 * BLOCK), lambda b, n: (jnp.minimum(n, 1), 0, 0, 0)),
            pl.BlockSpec(memory_space=pltpu.SMEM),
            whole(1, QGROUP),
            whole(1, QGROUP),
            whole(QGROUP, QGROUP),
            whole(1, SGU_WIDTH),
            whole(1, SGU_WIDTH),
            whole(SGU_GROUPS, CHUNK, CHUNK),
            whole(CHUNK, SGU_GROUPS),
        ] + cast_in,
        out_specs=[pl.BlockSpec((BLOCK, MIX_WIDTH), lambda b, n: (cur(b, n), 0))] + cast_out,
        out_shape=[jax.ShapeDtypeStruct((TOKENS, MIX_WIDTH), BF16)] + cast_shapes,
        compiler_params=_params(("arbitrary", "arbitrary"), 32 * MIB + cast_bytes),
        name="mixer",
    )(z, z, z, z, z, z, z, z, z, band_bias, sinks * LOG2E, q_gain, k_gain, head_ones,
      sgu_ln_w.reshape(1, SGU_WIDTH), sgu_ln_b.reshape(1, SGU_WIDTH), w_spatial, b_spatial.T, *cast_args)


def _mix_out_kernel(y_ref, w_ref, x_ref, mod_ref, o_ref):
    acc = jnp.dot(y_ref[...], w_ref[...], preferred_element_type=F32)
    o_ref[...] = x_ref[...] + mod_ref[2:3, :] * acc


def _mix_out(y, w_out, x, mod):
    tiles_per_seq = SEQ // MIX_OUT_ROWS
    blocks = (2 * MIX_OUT_ROWS * MIX_WIDTH * 2 + 2 * MIX_WIDTH * MIX_OUT_COLS * 2
              + 6 * MIX_OUT_ROWS * MIX_OUT_COLS * 4)
    return pl.pallas_call(
        _mix_out_kernel,
        grid=(TOKENS // MIX_OUT_ROWS, D_MODEL // MIX_OUT_COLS),
        in_specs=[
            pl.BlockSpec((MIX_OUT_ROWS, MIX_WIDTH), lambda i, j: (i, 0)),
            pl.BlockSpec((MIX_WIDTH, MIX_OUT_COLS), lambda i, j: (0, j)),
            pl.BlockSpec((MIX_OUT_ROWS, MIX_OUT_COLS), lambda i, j: (i, j)),
            pl.BlockSpec((None, N_MOD, MIX_OUT_COLS), lambda i, j: (i // tiles_per_seq, 0, j)),
        ],
        out_specs=pl.BlockSpec((MIX_OUT_ROWS, MIX_OUT_COLS), lambda i, j: (i, j)),
        out_shape=jax.ShapeDtypeStruct((TOKENS, D_MODEL), F32),
        compiler_params=_params(("arbitrary", "arbitrary"), blocks + 4 * MIB),
        name="mix_out",
    )(y, w_out, x, mod)


def _ffn_kernel(x_ref, xh_ref, mod_ref, nw_ref, wg_ref, wv_ref, cp_ref, wd_ref, *rest):
    i = pl.program_id(0)
    j = pl.program_id(1)
    h_ref, a0_ref, a1_ref, g0_ref, g1_ref = rest[-FFN_SCRATCH:]
    o_ref, cast_src, cast_dst = _split_cast_refs(rest[:-FFN_SCRATCH])
    a_refs, g_refs = (a0_ref, a1_ref), (g0_ref, g1_ref)
    side_cast = lambda: _cast_chunks(cast_src, cast_dst)

    gate_cols, val_cols = slice(0, FF_BLOCK), slice(FF_BLOCK, 2 * FF_BLOCK)

    def up(slot, rows=slice(None)):
        h = h_ref[rows, :]
        a_refs[slot][rows, gate_cols] = jnp.dot(h, wg_ref[...], preferred_element_type=F32)
        a_refs[slot][rows, val_cols] = jnp.dot(h, wv_ref[...], preferred_element_type=F32)

    def gate(slot, piece):
        a_ref, g_ref = a_refs[slot], g_refs[slot]
        rows = ROW_TILE // FFN_PIECES

        def conv(r0, cols, p0):
            lag = lambda k: a_ref[HALO - k + r0:HALO - k + r0 + GATE_ROWS, cols]
            return (lag(2) * cp_ref[p0:p0 + 1, :] + lag(1) * cp_ref[p0 + 1:p0 + 2, :]
                    + lag(0) * cp_ref[p0 + 2:p0 + 3, :] + cp_ref[p0 + 3:p0 + 4, :])

        for r0 in range(piece * rows, (piece + 1) * rows, GATE_ROWS):
            a_gate = conv(r0, gate_cols, 0)
            a_val = conv(r0, val_cols, CONV_WIDTH + 1)
            g_ref[r0:r0 + GATE_ROWS, :] = (a_gate * jax.nn.sigmoid(a_gate) * a_val).astype(BF16)

    def down(slot, piece):
        cols = slice(piece * (D_MODEL // FFN_PIECES), (piece + 1) * (D_MODEL // FFN_PIECES))
        return cols, jnp.dot(g_refs[slot][...], wd_ref[:, cols], preferred_element_type=F32)

    def gate_and_down(gate_slot, down_slot):
        for piece in range(FFN_PIECES):
            cols, acc = down(down_slot, piece)
            o_ref[:, cols] += acc
            gate(gate_slot, piece)

    last_slot = (N_FF - 1) % 2

    @pl.when(j == 0)
    def _():
        gain, shift = nw_ref[...] * (1.0 + mod_ref[4:5, :]), mod_ref[3:4, :]
        keep = jnp.where(i % (SEQ // ROW_TILE) == 0, 0.0, 1.0)
        h_ref[0:HALO, :] = (_adaln_rows(xh_ref[...], gain, shift) * keep).astype(BF16)
        o_ref[...] = jnp.zeros_like(o_ref)
        _norm_rows(x_ref, h_ref, HALO, 0, NORM_PIECE, gain, shift)
        for r0 in range(0, ROW_TILE, NORM_PIECE):
            up(0, slice(HALO + r0 if r0 else 0, HALO + r0 + NORM_PIECE))
            if r0 + NORM_PIECE < ROW_TILE:
                _norm_rows(x_ref, h_ref, HALO, r0 + NORM_PIECE, NORM_PIECE, gain, shift)
        side_cast()

    @pl.when(j == 1)
    def _():
        up(1)
        for piece in range(FFN_PIECES):
            gate(0, piece)
        side_cast()

    for slot in range(2):
        @pl.when((j >= 2) & (j < N_FF) & (lax.rem(j, 2) == slot))
        def _():
            gate_and_down(1 - slot, slot)
            up(slot)
            side_cast()

    @pl.when(j == N_FF)
    def _():
        gate_and_down(last_slot, 1 - last_slot)
        side_cast()

    @pl.when(j == N_FF + 1)
    def _():
        for piece in range(FFN_PIECES):
            cols, acc = down(last_slot, piece)
            o_ref[:, cols] = x_ref[:, cols] + mod_ref[5:6, cols] * (o_ref[:, cols] + acc)
        side_cast()


def _ffn(x, mod, norm_w, w_up, conv_w, conv_b, w_down, cast=None):
    tiles_per_seq = SEQ // ROW_TILE
    steps = N_FF + 2
    cast_in, cast_out, cast_shapes, cast_args, cast_bytes = _cast_plan(
        cast, TOKENS // ROW_TILE * steps, lambda i, j: i * steps + j)
    halo_blocks_per_tile = ROW_TILE // HALO
    conv_pack = jnp.concatenate([conv_w, conv_b.reshape(1, 2 * D_FF)], axis=0)
    conv_pack = conv_pack.reshape(CONV_WIDTH + 1, 2, N_FF, FF_BLOCK).transpose(2, 1, 0, 3)
    conv_pack = conv_pack.reshape(N_FF, 2 * (CONV_WIDTH + 1), FF_BLOCK)
    blk = lambda j, lag: jnp.clip(j - lag, 0, N_FF - 1)
    val = lambda b: N_FF + b
    ext_rows = HALO + ROW_TILE
    scratch = ext_rows * D_MODEL * 2 + 2 * ext_rows * 2 * FF_BLOCK * 4 + 2 * ROW_TILE * FF_BLOCK * 2
    blocks = 4 * ROW_TILE * D_MODEL * 4 + 6 * D_MODEL * FF_BLOCK * 2 + scratch + 6 * ext_rows * FF_BLOCK * 4
    return pl.pallas_call(
        _ffn_kernel,
        grid=(TOKENS // ROW_TILE, N_FF + 2),
        in_specs=[
            pl.BlockSpec((ROW_TILE, D_MODEL), lambda i, j: (i, 0)),
            pl.BlockSpec((HALO, D_MODEL), lambda i, j: (jnp.maximum(i * halo_blocks_per_tile - 1, 0), 0)),
            pl.BlockSpec((None, N_MOD, D_MODEL), lambda i, j: (i // tiles_per_seq, 0, 0)),
            pl.BlockSpec((1, D_MODEL), lambda i, j: (0, 0)),
            pl.BlockSpec((D_MODEL, FF_BLOCK), lambda i, j: (0, blk(j, 0))),
            pl.BlockSpec((D_MODEL, FF_BLOCK), lambda i, j: (0, val(blk(j, 0)))),
            pl.BlockSpec((None, 2 * (CONV_WIDTH + 1), FF_BLOCK), lambda i, j: (blk(j, 1), 0, 0)),
            pl.BlockSpec((FF_BLOCK, D_MODEL), lambda i, j: (blk(j, 2), 0)),
        ] + cast_in,
        out_specs=[pl.BlockSpec((ROW_TILE, D_MODEL), lambda i, j: (i, 0))] + cast_out,
        out_shape=[jax.ShapeDtypeStruct((TOKENS, D_MODEL), F32)] + cast_shapes,
        scratch_shapes=[pltpu.VMEM((ext_rows, D_MODEL), BF16)] + 2 * [pltpu.VMEM((ext_rows, 2 * FF_BLOCK), F32)]
        + 2 * [pltpu.VMEM((ROW_TILE, FF_BLOCK), BF16)],
        compiler_params=_params(("arbitrary", "arbitrary"), blocks + cast_bytes),
        name="ffn",
    )(x, x, mod, norm_w.reshape(1, D_MODEL), w_up, w_up, conv_pack, w_down, *cast_args)


def kernel(x, c, w_ada, b_ada, ada_table, rel_bias, norm_mix_w, w_in, q_norm_w, k_norm_w, sinks, sgu_ln_w,
           sgu_ln_b, w_spatial, b_spatial, w_out, norm_ffn_w, w_up, conv_w, conv_b, w_down):
    B, S, D = x.shape
    assert (B, S, D) == (BATCH, SEQ, D_MODEL)
    mods = _ada_mod(c, w_ada, b_ada, ada_table).reshape(DEPTH, BATCH, N_MOD, D_MODEL)
    band_bias = _band_bias(rel_bias)
    xt = x.reshape(TOKENS, D_MODEL)
    w_in_l, w_out_l, w_up_l, w_down_l = (w[0].astype(BF16) for w in (w_in, w_out, w_up, w_down))
    for l in range(DEPTH):
        mod, more = mods[l], l + 1 < DEPTH
        z = _mix_in(xt, mod, norm_mix_w[l], w_in_l)
        y, *cast_a = _mixer(z, band_bias, sinks[l], q_norm_w[l], k_norm_w[l], sgu_ln_w[l], sgu_ln_b[l],
                            w_spatial[l], b_spatial[l], cast=((w_in, w_up), l + 1) if more else None)
        xt = _mix_out(y, w_out_l, xt, mod)
        xt, *cast_b = _ffn(xt, mod, norm_ffn_w[l], w_up_l, conv_w[l], conv_b[l], w_down_l,
                           cast=((w_out, w_down), l + 1) if more else None)
        if more:
            (w_in_l, w_up_l), (w_out_l, w_down_l) = cast_a, cast_b
    return xt.reshape(B, S, D)
```
